```python
import functools
import jax
import jax.numpy as jnp
from jax import lax
import numpy as np

D_MODEL = 1024
BATCH = 2
SEQ = 8192
DEPTH = 2
DEC_BATCH = 32
DEC_SEQ = 8
PAST_LEN = 8192
PAGE_SIZE = 128

EPS = 1e-6
A_HEADS = 8
A_HEAD_DIM = 64
A_WIDTH = A_HEADS * A_HEAD_DIM
Q_BLOCK = 128
FORGET_BIAS = 3.0
RG_WIDTH = 512
RG_BLOCKS = 8
RG_BLOCK_DIM = RG_WIDTH // RG_BLOCKS
CONV_WIDTH = 4
LRU_C = 8.0
C_HEADS = 4
C_QK_DIM = 64
C_V_DIM = 128
C_QK_WIDTH = C_HEADS * C_QK_DIM
C_V_WIDTH = C_HEADS * C_V_DIM
RET_CHUNK = 128
ROPE_BASE = 10000.0
N_BRANCH = 3
MIX_WIDTH = A_WIDTH + RG_WIDTH + C_V_WIDTH
IN_SIZES = (A_WIDTH, A_WIDTH, A_WIDTH, A_HEADS,
            RG_WIDTH, RG_WIDTH,
            C_QK_WIDTH, C_QK_WIDTH, C_V_WIDTH, C_V_WIDTH,
            N_BRANCH * D_MODEL)
IN_WIDTH = A_WIDTH * 3 + A_HEADS + RG_WIDTH * 2 + C_QK_WIDTH * 2 + C_V_WIDTH * 2 + N_BRANCH * D_MODEL
N_EXPERTS = 32
TOP_K = 4
EXPERT_FF = D_MODEL
SWIGLU_LIMIT = 7.0
SWIGLU_ALPHA = 1.702
MOE_BLOCK = 256

kernel_name = 'fox_rglru_retention_moe_hybrid_step'


def _rmsnorm(x, g):
    x32 = x.astype(jnp.float32)
    y = x32 * lax.rsqrt(jnp.mean(x32 * x32, axis=-1, keepdims=True) + EPS)
    return (y * g.astype(jnp.float32)).astype(x.dtype)


def _split_in(z):
    parts = []
    off = 0
    for size in IN_SIZES:
        parts.append(z[..., off:off + size])
        off += size
    return parts


def _fox_prompt(q, k, v, logf):
    n, s, h, dh = q.shape
    nb = s // Q_BLOCK
    scale = dh ** -0.5
    cum = jnp.cumsum(logf, axis=1)
    cum_k = jnp.transpose(cum, (0, 2, 1))[:, :, None, :]
    k_pos = jnp.arange(s)
    q_blocks = jnp.moveaxis(q.reshape(n, nb, Q_BLOCK, h, dh), 1, 0)
    c_blocks = jnp.moveaxis(cum.reshape(n, nb, Q_BLOCK, h), 1, 0)

    def one_block(args):
        qb, cb, bi = args
        logits = jnp.einsum('nqhd,nkhd->nhqk', qb, k).astype(jnp.float32) * scale
        logits = logits + jnp.transpose(cb, (0, 2, 1))[..., None] - cum_k
        q_pos = bi * Q_BLOCK + jnp.arange(Q_BLOCK)
        logits = jnp.where(k_pos[None, :] <= q_pos[:, None], logits, -jnp.inf)
        p = jax.nn.softmax(logits, axis=-1).astype(v.dtype)
        return jnp.einsum('nhqk,nkhd->nqhd', p, v)

    out = lax.map(one_block, (q_blocks, c_blocks, jnp.arange(nb)))
    return jnp.moveaxis(out, 0, 1).reshape(n, s, h, dh)


def _fox_sample(q, k, v, logf, k_past, v_past, logf_past):
    n, l, h, dh = q.shape
    past = k_past.shape[1]
    scale = dh ** -0.5
    logf_past = logf_past.astype(jnp.float32)
    cum_new = jnp.cumsum(logf, axis=1)
    suffix_past = lax.cumsum(logf_past, axis=1, reverse=True) - logf_past
    key_bias = jnp.concatenate([suffix_past, -cum_new], axis=1)
    k_all = jnp.concatenate([k_past.astype(k.dtype), k], axis=1)
    v_all = jnp.concatenate([v_past.astype(v.dtype), v], axis=1)
    logits = jnp.einsum('nqhd,nkhd->nhqk', q, k_all).astype(jnp.float32) * scale
    logits = (logits + jnp.transpose(cum_new, (0, 2, 1))[..., None]
              + jnp.transpose(key_bias, (0, 2, 1))[:, :, None, :])
    mask = jnp.arange(past + l)[None, :] <= past + jnp.arange(l)[:, None]
    logits = jnp.where(mask, logits, -jnp.inf)
    p = jax.nn.softmax(logits, axis=-1).astype(v.dtype)
    return jnp.einsum('nhqk,nkhd->nqhd', p, v_all)


def _lin_comb(left, right):
    a_l, b_l = left
    a_r, b_r = right
    return a_l * a_r, a_r * b_l + b_r


def _rglru_branch(xb, gb, conv_state, h0, p):
    n, l, w = xb.shape
    xpad = jnp.concatenate([conv_state.astype(xb.dtype), xb], axis=1)
    conv_w = p['conv_w']
    xc = p['conv_b'] + sum(xpad[:, j:j + l] * conv_w[j] for j in range(CONV_WIDTH))
    conv_new = xpad[:, l:]
    xblk = xc.reshape(n, l, RG_BLOCKS, RG_BLOCK_DIM)
    r = jax.nn.sigmoid(jnp.einsum('nlbi,bij->nlbj', xblk, p['w_rg_a']).reshape(n, l, w) + p['b_rg_a'])
    i = jax.nn.sigmoid(jnp.einsum('nlbi,bij->nlbj', xblk, p['w_rg_i']).reshape(n, l, w) + p['b_rg_i'])
    log_a = (-LRU_C * r * jax.nn.softplus(-p['lru_lambda'])).astype(jnp.float32)
    a = jnp.exp(log_a)
    u = jnp.sqrt(-jnp.expm1(2.0 * log_a)) * (i * xc).astype(jnp.float32)
    a_cum, h = lax.associative_scan(_lin_comb, (a, u), axis=1)
    h = h + a_cum * h0.astype(jnp.float32)[:, None, :]
    y = h.astype(xb.dtype) * jax.nn.gelu(gb)
    return y, conv_new, h[:, -1]


def _rotary(x, pos):
    half = x.shape[-1] // 2
    inv = ROPE_BASE ** (-jnp.arange(half, dtype=jnp.float32) / half)
    ang = pos.astype(jnp.float32)[:, None] * inv[None, :]
    cos = jnp.cos(ang)[None, :, None, :]
    sin = jnp.sin(ang)[None, :, None, :]
    x1, x2 = x[..., :half], x[..., half:]
    return jnp.concatenate([x1 * cos - x2 * sin, x1 * sin + x2 * cos], axis=-1)


def _retention_branch(qc, kc, vc, gc, pos, s0):
    n, l, _ = qc.shape
    q = _rotary(qc.reshape(n, l, C_HEADS, C_QK_DIM).astype(jnp.float32), pos)
    k = _rotary(kc.reshape(n, l, C_HEADS, C_QK_DIM).astype(jnp.float32), pos) * (C_QK_DIM ** -0.5)
    v = vc.reshape(n, l, C_HEADS, C_V_DIM).astype(jnp.float32)
    t = RET_CHUNK if l % RET_CHUNK == 0 else l
    nc = l // t
    log_g = jnp.log1p(-jnp.exp2(-5.0 - jnp.arange(C_HEADS, dtype=jnp.float32)))
    idx = jnp.arange(t, dtype=jnp.float32)
    diff = idx[:, None] - idx[None, :]
    dmask = jnp.where(diff >= 0, jnp.exp(log_g[:, None, None] * jnp.maximum(diff, 0.0)), 0.0)
    qb = q.reshape(n, nc, t, C_HEADS, C_QK_DIM)
    kb = k.reshape(n, nc, t, C_HEADS, C_QK_DIM)
    vb = v.reshape(n, nc, t, C_HEADS, C_V_DIM)
    scores = jnp.einsum('ncthd,ncshd->nchts', qb, kb) * dmask
    o_inner = jnp.einsum('nchts,ncshe->ncthe', scores, vb)
    k_decay = jnp.exp(log_g[:, None] * (t - 1.0 - idx)[None, :])
    kv = jnp.einsum('ncshd,hs,ncshe->nchde', kb, k_decay, vb)
    chunk_decay = jnp.exp(log_g * t)[:, None, None]

    def step(s, kv_c):
        return chunk_decay * s + kv_c, s

    s_final, s_before = lax.scan(step, s0.astype(jnp.float32), jnp.moveaxis(kv, 1, 0))
    s_before = jnp.moveaxis(s_before, 0, 1)
    q_decay = jnp.exp(log_g[None, :] * (idx[:, None] + 1.0))
    o_cross = jnp.einsum('ncthd,nchde->ncthe', qb * q_decay[:, :, None], s_before)
    o = (o_inner + o_cross).reshape(n, l, C_HEADS, C_V_DIM)
    o = o * lax.rsqrt(jnp.mean(o * o, axis=-1, keepdims=True) + EPS)
    y = jax.nn.silu(gc.astype(jnp.float32)) * o.reshape(n, l, C_V_WIDTH)
    return y.astype(qc.dtype), s_final


def _moe(x, p):
    n, l, d = x.shape
    tokens = n * l
    xt = x.reshape(tokens, d)
    logits = (xt @ p['router_w'] + p['router_b']).astype(jnp.float32)
    top_v, top_i = lax.top_k(logits, TOP_K)
    weights = jax.nn.softmax(top_v, axis=-1)
    combine = jnp.sum(jax.nn.one_hot(top_i, N_EXPERTS, dtype=jnp.float32) * weights[..., None], axis=1)
    nb = -(-tokens // MOE_BLOCK)
    pad = nb * MOE_BLOCK - tokens
    xp = jnp.pad(xt, ((0, pad), (0, 0))).reshape(nb, MOE_BLOCK, d)
    cp = jnp.pad(combine, ((0, pad), (0, 0))).reshape(nb, MOE_BLOCK, N_EXPERTS)

    def one_block(args):
        xb, cb = args
        cb = cb.astype(xb.dtype)
        g = jnp.einsum('td,edf->tef', xb, p['w_gate_e']) + p['b_gate_e']
        u = jnp.einsum('td,edf->tef', xb, p['w_up_e']) + p['b_up_e']
        g = jnp.minimum(g, SWIGLU_LIMIT)
        u = jnp.clip(u, -SWIGLU_LIMIT, SWIGLU_LIMIT)
        hid = (u + 1.0) * g * jax.nn.sigmoid(SWIGLU_ALPHA * g) * cb[:, :, None]
        return jnp.einsum('tef,efd->td', hid, p['w_down_e']) + cb @ p['b_down_e']

    out = lax.map(one_block, (xp, cp)).reshape(nb * MOE_BLOCK, d)[:tokens]
    return out.reshape(n, l, d)


def _layer(x, c, pos, conv_state, h0, s0, fox_fn, p):
    n, l, _ = x.shape
    mod = jax.nn.silu(c) @ p['w_ada'] + p['b_ada']
    sh1, sc1, g1, sh2, sc2, g2 = [m[:, None, :] for m in jnp.split(mod, 6, axis=-1)]
    hn = _rmsnorm(x, p['norm_mix']) * (1.0 + sc1) + sh1
    z = hn @ p['w_in']
    qa, ka, va, fa, xb, gb, qc, kc, vc, gc, gl = _split_in(z)
    qa = qa.reshape(n, l, A_HEADS, A_HEAD_DIM)
    ka = ka.reshape(n, l, A_HEADS, A_HEAD_DIM)
    va = va.reshape(n, l, A_HEADS, A_HEAD_DIM)
    logf = jax.nn.log_sigmoid((fa + p['b_f']).astype(jnp.float32))
    oa = fox_fn(qa, ka, va, logf).reshape(n, l, A_WIDTH)
    ob, conv_new, h_new = _rglru_branch(xb, gb, conv_state, h0, p)
    oc, s_new = _retention_branch(qc, kc, vc, gc, pos, s0)
    gates = jax.nn.sigmoid(gl.reshape(n, l, N_BRANCH, D_MODEL).astype(jnp.float32)).astype(x.dtype)
    w_br = p['w_br']
    merged = (gates[:, :, 0] * (oa @ w_br[:A_WIDTH])
              + gates[:, :, 1] * (ob @ w_br[A_WIDTH:A_WIDTH + RG_WIDTH])
              + gates[:, :, 2] * (oc @ w_br[A_WIDTH + RG_WIDTH:]))
    x = x + g1 * (merged @ p['w_o'])
    hn2 = _rmsnorm(x, p['norm_ffn']) * (1.0 + sc2) + sh2
    x = x + g2 * _moe(hn2, p)
    return x, (ka, va, logf, conv_new, h_new, s_new)


def setup_inputs(seed: int = 0) -> dict:
    key = jax.random.key(seed)
    k = jax.random.split(key, 40)
    f32 = jnp.float32
    n_pages = PAST_LEN // PAGE_SIZE
    n_pool = (DEC_BATCH * n_pages * 5) // 4

    def nrm(i, shape, scale):
        return scale * jax.random.normal(k[i], shape, f32)

    page_table = jax.random.permutation(k[0], n_pool)[:DEC_BATCH * n_pages].reshape(DEC_BATCH, n_pages).astype(jnp.int32)
    a0 = jax.random.uniform(k[1], (DEPTH, RG_WIDTH), f32, minval=0.9, maxval=0.999) ** (1.0 / LRU_C)
    lru_lambda = jnp.log(a0) - jnp.log1p(-a0)
    return {
        'x_prompt': nrm(2, (BATCH, SEQ, D_MODEL), 1.0),
        'x_sample': nrm(3, (DEC_BATCH, DEC_SEQ, D_MODEL), 1.0),
        'c_prompt': nrm(4, (BATCH, D_MODEL), 1.0),
        'c_sample': nrm(5, (DEC_BATCH, D_MODEL), 1.0),
        'cache_k': nrm(6, (DEPTH, n_pool, PAGE_SIZE, A_HEADS, A_HEAD_DIM), 1.0),
        'cache_v': nrm(7, (DEPTH, n_pool, PAGE_SIZE, A_HEADS, A_HEAD_DIM), 1.0),
        'cache_logf': jax.nn.log_sigmoid(FORGET_BIAS + nrm(8, (DEPTH, n_pool, PAGE_SIZE, A_HEADS), 1.0)),
        'state_conv': nrm(9, (DEPTH, DEC_BATCH, CONV_WIDTH - 1, RG_WIDTH), 1.0),
        'state_h': nrm(10, (DEPTH, DEC_BATCH, RG_WIDTH), 0.5),
        'state_ret': nrm(11, (DEPTH, DEC_BATCH, C_HEADS, C_QK_DIM, C_V_DIM), 1.0),
        'page_table': page_table,
        'w_ada': nrm(12, (DEPTH, D_MODEL, 6 * D_MODEL), 0.5 * D_MODEL ** -0.5),
        'b_ada': nrm(13, (DEPTH, 6 * D_MODEL), 0.02),
        'norm_mix': 1.0 + nrm(14, (DEPTH, D_MODEL), 0.05),
        'norm_ffn': 1.0 + nrm(15, (DEPTH, D_MODEL), 0.05),
        'w_in': nrm(16, (DEPTH, D_MODEL, IN_WIDTH), D_MODEL ** -0.5),
        'b_f': FORGET_BIAS + nrm(17, (DEPTH, A_HEADS), 0.1),
        'conv_w': nrm(18, (DEPTH, CONV_WIDTH, RG_WIDTH), CONV_WIDTH ** -0.5),
        'conv_b': nrm(19, (DEPTH, RG_WIDTH), 0.01),
        'w_rg_a': nrm(20, (DEPTH, RG_BLOCKS, RG_BLOCK_DIM, RG_BLOCK_DIM), RG_BLOCK_DIM ** -0.5),
        'b_rg_a': nrm(21, (DEPTH, RG_WIDTH), 0.01),
        'w_rg_i': nrm(22, (DEPTH, RG_BLOCKS, RG_BLOCK_DIM, RG_BLOCK_DIM), RG_BLOCK_DIM ** -0.5),
        'b_rg_i': nrm(23, (DEPTH, RG_WIDTH), 0.01),
        'lru_lambda': lru_lambda,
        'w_br': nrm(24, (DEPTH, MIX_WIDTH, D_MODEL), A_WIDTH ** -0.5),
        'w_o': nrm(25, (DEPTH, D_MODEL, D_MODEL), D_MODEL ** -0.5),
        'router_w': nrm(26, (DEPTH, D_MODEL, N_EXPERTS), D_MODEL ** -0.5),
        'router_b': nrm(27, (DEPTH, N_EXPERTS), 0.01),
        'w_gate_e': nrm(28, (DEPTH, N_EXPERTS, D_MODEL, EXPERT_FF), D_MODEL ** -0.5),
        'b_gate_e': nrm(29, (DEPTH, N_EXPERTS, EXPERT_FF), 0.01),
        'w_up_e': nrm(30, (DEPTH, N_EXPERTS, D_MODEL, EXPERT_FF), D_MODEL ** -0.5),
        'b_up_e': nrm(31, (DEPTH, N_EXPERTS, EXPERT_FF), 0.01),
        'w_down_e': nrm(32, (DEPTH, N_EXPERTS, EXPERT_FF, D_MODEL), EXPERT_FF ** -0.5),
        'b_down_e': nrm(33, (DEPTH, N_EXPERTS, D_MODEL), 0.01),
        'final_norm': 1.0 + nrm(34, (D_MODEL,), 0.05),
    }


def reference(x_prompt, x_sample, c_prompt, c_sample, cache_k, cache_v, cache_logf, state_conv, state_h,
              state_ret, page_table, w_ada, b_ada, norm_mix, norm_ffn, w_in, b_f, conv_w, conv_b, w_rg_a,
              b_rg_a, w_rg_i, b_rg_i, lru_lambda, w_br, w_o, router_w, router_b, w_gate_e, b_gate_e, w_up_e,
              b_up_e, w_down_e, b_down_e, final_norm):
    n_p, seq, _ = x_prompt.shape
    n_s, dec_seq, _ = x_sample.shape
    past_len = page_table.shape[1] * cache_k.shape[2]
    pos_p = jnp.arange(seq)
    pos_s = past_len + jnp.arange(dec_seq)
    dt = x_prompt.dtype
    zero_conv = jnp.zeros((n_p, CONV_WIDTH - 1, RG_WIDTH), dt)
    zero_h = jnp.zeros((n_p, RG_WIDTH), jnp.float32)
    zero_s = jnp.zeros((n_p, C_HEADS, C_QK_DIM, C_V_DIM), jnp.float32)
    xp, xs = x_prompt, x_sample
    st_p, st_s = [], []
    for layer in range(DEPTH):
        p = {'w_ada': w_ada[layer], 'b_ada': b_ada[layer], 'norm_mix': norm_mix[layer],
             'norm_ffn': norm_ffn[layer], 'w_in': w_in[layer], 'b_f': b_f[layer], 'conv_w': conv_w[layer],
             'conv_b': conv_b[layer], 'w_rg_a': w_rg_a[layer], 'b_rg_a': b_rg_a[layer],
             'w_rg_i': w_rg_i[layer], 'b_rg_i': b_rg_i[layer], 'lru_lambda': lru_lambda[layer],
             'w_br': w_br[layer], 'w_o': w_o[layer], 'router_w': router_w[layer], 'router_b': router_b[layer],
             'w_gate_e': w_gate_e[layer], 'b_gate_e': b_gate_e[layer], 'w_up_e': w_up_e[layer],
             'b_up_e': b_up_e[layer], 'w_down_e': w_down_e[layer], 'b_down_e': b_down_e[layer]}
        xp, sp = _layer(xp, c_prompt, pos_p, zero_conv, zero_h, zero_s, _fox_prompt, p)
        k_past = cache_k[layer][page_table].reshape(n_s, past_len, A_HEADS, A_HEAD_DIM)
        v_past = cache_v[layer][page_table].reshape(n_s, past_len, A_HEADS, A_HEAD_DIM)
        f_past = cache_logf[layer][page_table].reshape(n_s, past_len, A_HEADS)
        fox_s = functools.partial(_fox_sample, k_past=k_past, v_past=v_past, logf_past=f_past)
        xs, ss = _layer(xs, c_sample, pos_s, state_conv[layer], state_h[layer], state_ret[layer], fox_s, p)
        st_p.append(sp)
        st_s.append(ss)
    y_prompt = _rmsnorm(xp, final_norm)
    y_sample = _rmsnorm(xs, final_norm)
    new_k_prompt = jnp.stack([s[0] for s in st_p])
    new_v_prompt = jnp.stack([s[1] for s in st_p])
    new_logf_prompt = jnp.stack([s[2] for s in st_p])
    new_conv_prompt = jnp.stack([s[3] for s in st_p])
    new_h_prompt = jnp.stack([s[4] for s in st_p])
    new_ret_prompt = jnp.stack([s[5] for s in st_p])
    new_k_sample = jnp.stack([s[0] for s in st_s])
    new_v_sample = jnp.stack([s[1] for s in st_s])
    new_logf_sample = jnp.stack([s[2] for s in st_s])
    new_conv_sample = jnp.stack([s[3] for s in st_s])
    new_h_sample = jnp.stack([s[4] for s in st_s])
    new_ret_sample = jnp.stack([s[5] for s in st_s])
    return (y_prompt, y_sample, new_k_prompt, new_v_prompt, new_logf_prompt, new_conv_prompt, new_h_prompt,
            new_ret_prompt, new_k_sample, new_v_sample, new_logf_sample, new_conv_sample, new_h_sample,
            new_ret_sample)
```

```python
import functools
import math

import jax
import jax.numpy as jnp
from jax import lax
from jax.experimental import pallas as pl
from jax.experimental.pallas import tpu as pltpu

F32 = jnp.float32
BF16 = jnp.bfloat16

EPS = 1e-6
A_HEADS = 8
A_HEAD_DIM = 64
A_WIDTH = A_HEADS * A_HEAD_DIM
RG_WIDTH = 512
CONV_WIDTH = 4
LRU_C = 8.0
C_HEADS = 4
C_QK_DIM = 64
C_V_DIM = 128
C_QK_WIDTH = C_HEADS * C_QK_DIM
C_V_WIDTH = C_HEADS * C_V_DIM
RET_CHUNK = 128
ROPE_BASE = 10000.0
N_BRANCH = 3
TOP_K = 4
SWIGLU_LIMIT = 7.0
SWIGLU_ALPHA = 1.702

LANES = 128
SUBLANES = 8
VMEM_LIMIT = 56 * 1024 * 1024

ROW_TILE = 256
FLASH_TILE = 256
SCAN_TILE = 256
PAGES_PER_STEP = 16
HEADS_PER_STEP = 2

_IN_LAYOUT = (("qa", A_WIDTH), ("ka", A_WIDTH), ("va", A_WIDTH), ("xb", RG_WIDTH), ("gb", RG_WIDTH),
              ("qc", C_QK_WIDTH), ("kc", C_QK_WIDTH), ("vc", C_V_WIDTH), ("gc", C_V_WIDTH),
              ("gl", None), ("fa", LANES))


def _cparams(*sem):
    return pltpu.CompilerParams(dimension_semantics=sem, vmem_limit_bytes=VMEM_LIMIT)


def _dot(a, b):
    return jnp.dot(a, b, preferred_element_type=F32)


def _dot_nt(a, b):
    return lax.dot_general(a, b, (((1,), (1,)), ((), ())), preferred_element_type=F32)


def _dot_tn(a, b):
    return lax.dot_general(a, b, (((0,), (0,)), ((), ())), preferred_element_type=F32)


def _log_sigmoid(x):
    return jnp.minimum(x, 0.0) - jnp.log1p(jnp.exp(-jnp.abs(x)))


def _softplus(x):
    return jnp.maximum(x, 0.0) + jnp.log1p(jnp.exp(-jnp.abs(x)))


def _rms(x):
    return x * lax.rsqrt(jnp.mean(x * x, axis=-1, keepdims=True) + EPS)


def _expand_groups(m, rows):
    g, d = m.shape
    if g == 1:
        return m
    return jnp.broadcast_to(m[:, None, :], (g, rows // g, d)).reshape(rows, d)


def _adaln_kernel(c_ref, w_ref, b_ref, o_ref):
    c = c_ref[...]
    s = (c * jax.nn.sigmoid(c)).astype(BF16)
    o_ref[0] = _dot(s, w_ref[0]) + b_ref[0]


def _adaln(c_all, w_ada, b_ada):
    depth, d, six_d = w_ada.shape
    r = c_all.shape[0]
    nj = six_d // d
    return pl.pallas_call(
        _adaln_kernel,
        grid=(depth, nj),
        in_specs=[pl.BlockSpec((r, d), lambda l, j: (0, 0)),
                  pl.BlockSpec((1, d, d), lambda l, j: (l, 0, j)),
                  pl.BlockSpec((1, 1, d), lambda l, j: (l, 0, j))],
        out_specs=pl.BlockSpec((1, r, d), lambda l, j: (l, 0, j)),
        out_shape=jax.ShapeDtypeStruct((depth, r, six_d), F32),
        compiler_params=_cparams("arbitrary", "arbitrary"),
        name="adaln",
    )(c_all, w_ada, b_ada.reshape(depth, 1, six_d))


def _norm_mod_kernel(x_ref, g_ref, sc_ref, sh_ref, o_ref):
    x = x_ref[...]
    rows = x.shape[0]
    y = _rms(x) * g_ref[...]
    y = y * (1.0 + _expand_groups(sc_ref[0], rows)) + _expand_groups(sh_ref[0], rows)
    o_ref[...] = y.astype(BF16)


def _norm_mod(x, g, sc, sh, tm):
    r, d = x.shape
    nt, gq, _ = sc.shape
    return pl.pallas_call(
        _norm_mod_kernel,
        grid=(nt,),
        in_specs=[pl.BlockSpec((tm, d), lambda i: (i, 0)),
                  pl.BlockSpec((1, d), lambda i: (0, 0)),
                  pl.BlockSpec((1, gq, d), lambda i: (i, 0, 0)),
                  pl.BlockSpec((1, gq, d), lambda i: (i, 0, 0))],
        out_specs=pl.BlockSpec((tm, d), lambda i: (i, 0)),
        out_shape=jax.ShapeDtypeStruct((r, d), BF16),
        compiler_params=_cparams("arbitrary"),
        name="norm_mod",
    )(x, g.reshape(1, d), sc, sh)


def _in_proj_kernel(offs, head_major, x_ref, w_ref, bf_ref, *outs):
    x = x_ref[...]

    def mm(name):
        a, b = offs[name]
        return _dot(x, w_ref[:, a:b])

    if head_major:
        (qrow, kf, vf, xb, gb, qc, kc, vc, gc, gl, lf, qhm, khm, vhm) = outs
    else:
        (qrow, kf, vf, xb, gb, qc, kc, vc, gc, gl, lf) = outs
    q = mm("qa").astype(BF16)
    qrow[...] = q
    k = mm("ka")
    kf[...] = k
    v = mm("va")
    vf[...] = v
    if head_major:
        kb = k.astype(BF16)
        vb = v.astype(BF16)
        for h in range(A_HEADS):
            sl = slice(h * A_HEAD_DIM, (h + 1) * A_HEAD_DIM)
            qhm[h] = q[:, sl]
            khm[h] = kb[:, sl]
            vhm[h] = vb[:, sl]
    xb[...] = mm("xb")
    gb[...] = mm("gb")
    qc[...] = mm("qc")
    kc[...] = mm("kc")
    vc[...] = mm("vc").astype(BF16)
    gc[...] = mm("gc")
    gl[...] = mm("gl")
    lf[...] = _log_sigmoid(mm("fa") + bf_ref[...])


def _in_proj(hn, w_r, bf_pad, offs, head_major, tm):
    r, d = hn.shape
    nc = w_r.shape[1]
    gl_w = offs["gl"][1] - offs["gl"][0]
    row = lambda w, dt: (pl.BlockSpec((tm, w), lambda i: (i, 0)), jax.ShapeDtypeStruct((r, w), dt))
    outs = [row(A_WIDTH, BF16), row(A_WIDTH, F32), row(A_WIDTH, F32), row(RG_WIDTH, F32), row(RG_WIDTH, F32),
            row(C_QK_WIDTH, F32), row(C_QK_WIDTH, F32), row(C_V_WIDTH, BF16), row(C_V_WIDTH, F32),
            row(gl_w, F32), row(LANES, F32)]
    if head_major:
        hm = (pl.BlockSpec((A_HEADS, tm, A_HEAD_DIM), lambda i: (0, i, 0)),
              jax.ShapeDtypeStruct((A_HEADS, r, A_HEAD_DIM), BF16))
        outs += [hm, hm, hm]
    return pl.pallas_call(
        functools.partial(_in_proj_kernel, offs, head_major),
        grid=(r // tm,),
        in_specs=[pl.BlockSpec((tm, d), lambda i: (i, 0)),
                  pl.BlockSpec((d, nc), lambda i: (0, 0), pipeline_mode=pl.Buffered(1)),
                  pl.BlockSpec((1, LANES), lambda i: (0, 0))],
        out_specs=[o[0] for o in outs],
        out_shape=[o[1] for o in outs],
        compiler_params=_cparams("arbitrary"),
        name="in_proj",
    )(hn, w_r, bf_pad)


def _cumsum_lanes_kernel(x_ref, o_ref):
    c = x_ref[...]
    n = c.shape[1]
    lane = lax.broadcasted_iota(jnp.int32, c.shape, 1)
    s = 1
    while s < n:
        c = c + jnp.where(lane >= s, pltpu.roll(c, s, axis=1), 0.0)
        s *= 2
    o_ref[...] = c


def _cumsum_lanes(x):
    return pl.pallas_call(
        _cumsum_lanes_kernel,
        out_shape=jax.ShapeDtypeStruct(x.shape, F32),
        compiler_params=pltpu.CompilerParams(vmem_limit_bytes=VMEM_LIMIT),
        name="fox_cumsum",
    )(x)


def _flash_kernel(t, q_ref, k_ref, v_ref, cq_ref, ck_ref, o_ref):
    qi = pl.program_id(2)
    row = lax.broadcasted_iota(jnp.int32, (t, t), 0)
    col = lax.broadcasted_iota(jnp.int32, (t, t), 1)
    outs = []
    for j in range(HEADS_PER_STEP):
        q = q_ref[j]
        cq = cq_ref[0, j]

        def scores(kb):
            start = pl.multiple_of(kb * t, t)
            k = k_ref[j, pl.ds(start, t), :]
            ck = ck_ref[0, j, :, pl.ds(start, t)]
            return _dot_nt(q, k) + cq - ck, start

        def update(carry, s, start):
            m, l, acc = carry
            m_new = jnp.maximum(m, jnp.max(s, axis=-1, keepdims=True))
            p = jnp.exp(s - m_new)
            alpha = jnp.exp(m - m_new)
            l = alpha * l + jnp.sum(p, axis=-1, keepdims=True)
            v = v_ref[j, pl.ds(start, t), :]
            acc = alpha * acc + _dot(p.astype(BF16), v)
            return m_new, l, acc

        def body(kb, carry):
            s, start = scores(kb)
            return update(carry, s, start)

        init = (jnp.full((t, 1), -jnp.inf, F32), jnp.zeros((t, 1), F32), jnp.zeros((t, A_HEAD_DIM), F32))
        carry = lax.fori_loop(0, qi, body, init)
        s, start = scores(qi)
        s = jnp.where(col <= row, s, -jnp.inf)
        m, l, acc = update(carry, s, start)
        outs.append(acc / l)
    o_ref[...] = jnp.concatenate(outs, axis=1).astype(BF16)


def _fox_prompt(q_hm, k_hm, v_hm, cum, n, s):
    t = min(FLASH_TILE, s)
    nq = s // t
    hp = HEADS_PER_STEP
    cum_col = cum.reshape(n, A_HEADS, s, 1)
    cum_row = cum.reshape(n, A_HEADS, 1, s)
    return pl.pallas_call(
        functools.partial(_flash_kernel, t),
        grid=(n, A_HEADS // hp, nq),
        in_specs=[pl.BlockSpec((hp, t, A_HEAD_DIM), lambda b, h, i: (h, b * nq + i, 0)),
                  pl.BlockSpec((hp, s, A_HEAD_DIM), lambda b, h, i: (h, b, 0)),
                  pl.BlockSpec((hp, s, A_HEAD_DIM), lambda b, h, i: (h, b, 0)),
                  pl.BlockSpec((1, hp, t, 1), lambda b, h, i: (b, h, i, 0)),
                  pl.BlockSpec((1, hp, 1, s), lambda b, h, i: (b, h, 0, 0))],
        out_specs=pl.BlockSpec((t, hp * A_HEAD_DIM), lambda b, h, i: (b * nq + i, h)),
        out_shape=jax.ShapeDtypeStruct((n * s, A_WIDTH), BF16),
        compiler_params=_cparams("arbitrary", "arbitrary", "arbitrary"),
        name="fox_prompt",
    )(q_hm, k_hm, v_hm, cum_col, cum_row)


def _paged_kernel(pps, page, pt_ref, q_ref, kn_ref, vn_ref, lfn_ref, *rest):
    k_refs = rest[:pps]
    v_refs = rest[pps:2 * pps]
    lf_refs = rest[2 * pps:3 * pps]
    o_ref = rest[3 * pps]
    qbd_ref, cq_ref, negc_ref, m_ref, l_ref, acc_ref, run_ref = rest[3 * pps + 1:]
    j = pl.program_id(1)
    nj = pl.num_programs(1)
    l_new = q_ref.shape[1]
    rows = A_HEADS * l_new
    lane_w = lax.broadcasted_iota(jnp.int32, (l_new, A_WIDTH), 1)
    row_i = lax.broadcasted_iota(jnp.int32, (rows, page), 0)
    lane_i = lax.broadcasted_iota(jnp.int32, (rows, page), 1)
    tok_of_row = row_i % l_new

    def expand(x):
        return jnp.broadcast_to(x[:, None, :], (A_HEADS, l_new, page)).reshape(rows, page)

    @pl.when(j == 0)
    def _():
        q = q_ref[0].astype(F32)
        qbd_ref[...] = jnp.concatenate(
            [jnp.where((lane_w // A_HEAD_DIM) == h, q, 0.0) for h in range(A_HEADS)], axis=0).astype(BF16)
        c = lfn_ref[0]
        lane = lax.broadcasted_iota(jnp.int32, c.shape, 1)
        s = 1
        while s < l_new:
            c = c + jnp.where(lane >= s, pltpu.roll(c, s, axis=1), 0.0)
            s *= 2
        ce = expand(c)
        negc_ref[...] = -ce
        cq_ref[...] = jnp.sum(jnp.where(lane_i == tok_of_row, ce, 0.0), axis=-1, keepdims=True)
        m_ref[...] = jnp.full(m_ref.shape, -jnp.inf, F32)
        l_ref[...] = jnp.zeros(l_ref.shape, F32)
        acc_ref[...] = jnp.zeros(acc_ref.shape, F32)
        run_ref[...] = jnp.zeros(run_ref.shape, F32)

    qbd = qbd_ref[...]
    cq = cq_ref[...]

    def update(s, v):
        m = m_ref[...]
        m_new = jnp.maximum(m, jnp.max(s, axis=-1, keepdims=True))
        p = jnp.exp(s - m_new)
        alpha = jnp.exp(m - m_new)
        l_ref[...] = alpha * l_ref[...] + jnp.sum(p, axis=-1, keepdims=True)
        acc_ref[...] = alpha * acc_ref[...] + _dot(p.astype(BF16), v)
        m_ref[...] = m_new

    lane_p = lax.broadcasted_iota(jnp.int32, (A_HEADS, page), 1)
    for i in reversed(range(pps)):
        lf = lf_refs[i][0]
        x = lf
        s = 1
        while s < page:
            x = x + jnp.where(lane_p < page - s, pltpu.roll(x, page - s, axis=1), 0.0)
            s *= 2
        run = run_ref[...]
        suffix = x - lf + run
        run_ref[...] = run + x[:, 0:1]
        kp = k_refs[i][0].astype(BF16)
        vp = v_refs[i][0].astype(BF16)
        update(_dot_nt(qbd, kp) + expand(suffix) + cq, vp)

    @pl.when(j == nj - 1)
    def _():
        pad = jnp.zeros((page - l_new, A_WIDTH), BF16)
        kn = jnp.concatenate([kn_ref[0].astype(BF16), pad], axis=0)
        vn = jnp.concatenate([vn_ref[0].astype(BF16), pad], axis=0)
        s = _dot_nt(qbd, kn) + negc_ref[...] + cq
        s = jnp.where(lane_i <= tok_of_row, s, -jnp.inf)
        update(s, vn)
        o = acc_ref[...] / l_ref[...]
        out = jnp.zeros((l_new, A_WIDTH), F32)
        for h in range(A_HEADS):
            sel = (lane_w // A_HEAD_DIM) == h
            out = out + jnp.where(sel, o[h * l_new:(h + 1) * l_new, :], 0.0)
        o_ref[0] = out.astype(BF16)


def _fox_sample(q, k_new, v_new, lfn_t, cache_k, cache_v, cache_lf_t, page_table):
    b, l_new, _ = q.shape
    page = cache_k.shape[1]
    n_pages = page_table.shape[1]
    pps = math.gcd(PAGES_PER_STEP, n_pages)
    nj = n_pages // pps
    rows = A_HEADS * l_new

    def page_map(i):
        return lambda bi, j, pt: (pt[bi * n_pages + (nj - 1 - j) * pps + i], 0, 0)

    seq = lambda w: pl.BlockSpec((1, l_new, w), lambda bi, j, pt: (bi, 0, 0))
    in_specs = [seq(A_WIDTH), seq(A_WIDTH), seq(A_WIDTH),
                pl.BlockSpec((1, A_HEADS, page), lambda bi, j, pt: (bi, 0, 0))]
    in_specs += [pl.BlockSpec((1, page, A_WIDTH), page_map(i)) for i in range(pps)]
    in_specs += [pl.BlockSpec((1, page, A_WIDTH), page_map(i)) for i in range(pps)]
    in_specs += [pl.BlockSpec((1, A_HEADS, page), page_map(i)) for i in range(pps)]
    grid_spec = pltpu.PrefetchScalarGridSpec(
        num_scalar_prefetch=1,
        grid=(b, nj),
        in_specs=in_specs,
        out_specs=pl.BlockSpec((1, l_new, A_WIDTH), lambda bi, j, pt: (bi, 0, 0)),
        scratch_shapes=[pltpu.VMEM((rows, A_WIDTH), BF16), pltpu.VMEM((rows, 1), F32),
                        pltpu.VMEM((rows, page), F32), pltpu.VMEM((rows, 1), F32),
                        pltpu.VMEM((rows, 1), F32), pltpu.VMEM((rows, A_WIDTH), F32),
                        pltpu.VMEM((A_HEADS, 1), F32)],
    )
    return pl.pallas_call(
        functools.partial(_paged_kernel, pps, page),
        grid_spec=grid_spec,
        out_shape=jax.ShapeDtypeStruct((b, l_new, A_WIDTH), BF16),
        compiler_params=_cparams("arbitrary", "arbitrary"),
        name="fox_sample",
    )(page_table.reshape(-1), q, k_new, v_new, lfn_t,
      *([cache_k] * pps), *([cache_v] * pps), *([cache_lf_t] * pps))


def _rglru_kernel(tl, xb_ref, gb_ref, cs_ref, h0_ref, cw_ref, cb_ref, wa_ref, ba_ref, wi_ref, bi_ref, lam_ref,
                  y_ref, cn_ref, hn_ref, xpad, hcar):
    i = pl.program_id(1)
    hist = CONV_WIDTH - 1
    base = SUBLANES

    @pl.when(i == 0)
    def _():
        xpad[base - hist:base, :] = cs_ref[0]
        hcar[...] = h0_ref[0]

    x = xb_ref[...]
    xpad[base:base + tl, :] = x
    xc = cb_ref[...] + sum(xpad[base - hist + j:base - hist + j + tl, :] * cw_ref[j:j + 1, :]
                           for j in range(CONV_WIDTH))
    tail = xpad[base + tl - hist:base + tl, :]
    xpad[base - hist:base, :] = tail
    cn_ref[0] = tail

    xcb = xc.astype(BF16)
    r = jax.nn.sigmoid(_dot(xcb, wa_ref[...]) + ba_ref[...])
    ig = jax.nn.sigmoid(_dot(xcb, wi_ref[...]) + bi_ref[...])
    log_a = -LRU_C * r * _softplus(-lam_ref[...])
    a = jnp.exp(log_a)
    x2 = 2.0 * log_a
    e2 = jnp.exp(x2)
    em1 = jnp.where(e2 == 1.0, x2, jnp.where(x2 < -1.0, e2 - 1.0, (e2 - 1.0) * x2 / jnp.log(e2)))
    u = jnp.sqrt(-em1) * (ig * xc)

    row = lax.broadcasted_iota(jnp.int32, (tl, RG_WIDTH), 0)
    aa, bb = a, u
    s = 1
    while s < tl:
        keep = row >= s
        a_sh = jnp.where(keep, pltpu.roll(aa, s, axis=0), 1.0)
        b_sh = jnp.where(keep, pltpu.roll(bb, s, axis=0), 0.0)
        bb = aa * b_sh + bb
        aa = aa * a_sh
        s *= 2
    h = aa * hcar[...] + bb
    hlast = h[tl - 1:tl, :]
    hcar[...] = hlast
    hn_ref[0] = hlast
    y_ref[...] = (h * jax.nn.gelu(gb_ref[...])).astype(BF16)


def _rglru(xb, gb, conv_state, h0, cw, cb, wa_bd, ba, wi_bd, bi, lam, n, l):
    tl = min(SCAN_TILE, l)
    nt = l // tl
    w = RG_WIDTH
    hist = CONV_WIDTH - 1
    full = lambda shp: pl.BlockSpec(shp, lambda b, i: tuple(0 for _ in shp))
    return pl.pallas_call(
        functools.partial(_rglru_kernel, tl),
        grid=(n, nt),
        in_specs=[pl.BlockSpec((tl, w), lambda b, i: (b * nt + i, 0)),
                  pl.BlockSpec((tl, w), lambda b, i: (b * nt + i, 0)),
                  pl.BlockSpec((1, hist, w), lambda b, i: (b, 0, 0)),
                  pl.BlockSpec((1, 1, w), lambda b, i: (b, 0, 0)),
                  full((CONV_WIDTH, w)), full((1, w)), full((w, w)), full((1, w)), full((w, w)), full((1, w)),
                  full((1, w))],
        out_specs=[pl.BlockSpec((tl, w), lambda b, i: (b * nt + i, 0)),
                   pl.BlockSpec((1, hist, w), lambda b, i: (b, 0, 0)),
                   pl.BlockSpec((1, 1, w), lambda b, i: (b, 0, 0))],
        out_shape=[jax.ShapeDtypeStruct((n * l, w), BF16),
                   jax.ShapeDtypeStruct((n, hist, w), F32),
                   jax.ShapeDtypeStruct((n, 1, w), F32)],
        scratch_shapes=[pltpu.VMEM((tl + SUBLANES, w), F32), pltpu.VMEM((1, w), F32)],
        compiler_params=_cparams("arbitrary", "arbitrary"),
        name="rglru",
    )(xb, gb, conv_state, h0.reshape(n, 1, w), cw, cb.reshape(1, w), wa_bd, ba.reshape(1, w), wi_bd,
      bi.reshape(1, w), lam.reshape(1, w))


def _retention_kernel(t, qc_ref, kc_ref, vc_ref, gc_ref, cos_ref, sin_ref, qdec_ref, kdec_ref, dmask_ref,
                      cdec_ref, s0_ref, y_ref, sn_ref, st):
    c = pl.program_id(1)
    nc = pl.num_programs(1)

    @pl.when(c == 0)
    def _():
        st[...] = s0_ref[0]

    lane = lax.broadcasted_iota(jnp.int32, (t, C_QK_WIDTH), 1)
    first_half = (lane % C_QK_DIM) < (C_QK_DIM // 2)
    cos = cos_ref[...]
    sin = sin_ref[...]
    half = C_QK_DIM // 2

    def rot(x):
        swapped = jnp.where(first_half, pltpu.roll(x, C_QK_WIDTH - half, axis=1), pltpu.roll(x, half, axis=1))
        return x * cos + swapped * sin

    qr = rot(qc_ref[...])
    kr = rot(kc_ref[...]) * (C_QK_DIM ** -0.5)
    qb = qr.astype(BF16)
    kb = kr.astype(BF16)
    qd = (qr * qdec_ref[...]).astype(BF16)
    kd = (kr * kdec_ref[...]).astype(BF16)
    v = vc_ref[...]
    outs = []
    for h in range(C_HEADS):
        qs = slice(h * C_QK_DIM, (h + 1) * C_QK_DIM)
        vh = v[:, h * C_V_DIM:(h + 1) * C_V_DIM]
        sc = _dot_nt(qb[:, qs], kb[:, qs]) * dmask_ref[h]
        sh = st[h]
        o = _dot(sc.astype(BF16), vh) + _dot(qd[:, qs], sh.astype(BF16))
        st[h] = cdec_ref[h] * sh + _dot_tn(kd[:, qs], vh)
        outs.append(_rms(o))
    g = gc_ref[...]
    y_ref[...] = ((g * jax.nn.sigmoid(g)) * jnp.concatenate(outs, axis=1)).astype(BF16)

    @pl.when(c == nc - 1)
    def _():
        sn_ref[0] = st[...]


def _retention_tables(l, pos0):
    t = RET_CHUNK if l % RET_CHUNK == 0 else l
    half = C_QK_DIM // 2
    inv = ROPE_BASE ** (-jnp.arange(half, dtype=F32) / half)
    ang = (pos0 + jnp.arange(l)).astype(F32)[:, None] * inv[None, :]
    cos = jnp.tile(jnp.cos(ang), (1, 2 * C_HEADS))
    sin = jnp.tile(jnp.concatenate([-jnp.sin(ang), jnp.sin(ang)], axis=1), (1, C_HEADS))
    log_g = jnp.log1p(-jnp.exp2(-5.0 - jnp.arange(C_HEADS, dtype=F32)))
    idx = jnp.arange(t, dtype=F32)
    diff = idx[:, None] - idx[None, :]
    dmask = jnp.where(diff >= 0, jnp.exp(log_g[:, None, None] * jnp.maximum(diff, 0.0)), 0.0)
    kdec = jnp.repeat(jnp.exp(log_g[None, :] * (t - 1.0 - idx)[:, None]), C_QK_DIM, axis=1)
    qdec = jnp.repeat(jnp.exp(log_g[None, :] * (idx[:, None] + 1.0)), C_QK_DIM, axis=1)
    cdec = jnp.broadcast_to(jnp.exp(log_g * t)[:, None, None], (C_HEADS, 1, C_V_DIM))
    return t, cos, sin, qdec, kdec, dmask, cdec


def _retention(qc, kc, vc, gc, s0, n, l, pos0):
    t, cos, sin, qdec, kdec, dmask, cdec = _retention_tables(l, pos0)
    nc = l // t
    full = lambda shp: pl.BlockSpec(shp, lambda b, c: tuple(0 for _ in shp))
    rowspec = lambda w: pl.BlockSpec((t, w), lambda b, c: (b * nc + c, 0))
    return pl.pallas_call(
        functools.partial(_retention_kernel, t),
        grid=(n, nc),
        in_specs=[rowspec(C_QK_WIDTH), rowspec(C_QK_WIDTH), rowspec(C_V_WIDTH), rowspec(C_V_WIDTH),
                  pl.BlockSpec((t, C_QK_WIDTH), lambda b, c: (c, 0)),
                  pl.BlockSpec((t, C_QK_WIDTH), lambda b, c: (c, 0)),
                  full((t, C_QK_WIDTH)), full((t, C_QK_WIDTH)), full((C_HEADS, t, t)),
                  full((C_HEADS, 1, C_V_DIM)),
                  pl.BlockSpec((1, C_HEADS, C_QK_DIM, C_V_DIM), lambda b, c: (b, 0, 0, 0))],
        out_specs=[rowspec(C_V_WIDTH),
                   pl.BlockSpec((1, C_HEADS, C_QK_DIM, C_V_DIM), lambda b, c: (b, 0, 0, 0))],
        out_shape=[jax.ShapeDtypeStruct((n * l, C_V_WIDTH), BF16),
                   jax.ShapeDtypeStruct((n, C_HEADS, C_QK_DIM, C_V_DIM), F32)],
        scratch_shapes=[pltpu.VMEM((C_HEADS, C_QK_DIM, C_V_DIM), F32)],
        compiler_params=_cparams("arbitrary", "arbitrary"),
        name="retention",
    )(qc, kc, vc, gc, cos, sin, qdec, kdec, dmask, cdec, s0)


def _merge_kernel(n_exp, oa_ref, ob_ref, oc_ref, gl_ref, x_ref, g1_ref, sc_ref, sh_ref, nf_ref, wbr_ref, wo_ref,
                  wr_ref, br_ref, xm_ref, hn_ref, ti_ref, tw_ref):
    rows, d = x_ref.shape
    m = jnp.zeros((rows, d), F32)
    for b, (o_ref, w0) in enumerate(((oa_ref, 0), (ob_ref, A_WIDTH), (oc_ref, A_WIDTH + RG_WIDTH))):
        width = o_ref.shape[1]
        gate = jax.nn.sigmoid(gl_ref[:, b * d:(b + 1) * d])
        m = m + gate * _dot(o_ref[...], wbr_ref[w0:w0 + width, :])
    x = x_ref[...] + _expand_groups(g1_ref[0], rows) * _dot(m.astype(BF16), wo_ref[...])
    xm_ref[...] = x
    hn = _rms(x) * nf_ref[...]
    hn = (hn * (1.0 + _expand_groups(sc_ref[0], rows)) + _expand_groups(sh_ref[0], rows)).astype(BF16)
    hn_ref[...] = hn
    lane = lax.broadcasted_iota(jnp.int32, (rows, LANES), 1)
    logits = jnp.where(lane < n_exp, _dot(hn, wr_ref[...]) + br_ref[...], -jnp.inf)
    vals, idxs = [], []
    for _ in range(TOP_K):
        mx = jnp.max(logits, axis=-1, keepdims=True)
        ix = jnp.min(jnp.where(logits == mx, lane, LANES), axis=-1, keepdims=True)
        vals.append(mx)
        idxs.append(ix)
        logits = jnp.where(lane == ix, -jnp.inf, logits)
    es = [jnp.exp(v - vals[0]) for v in vals]
    den = es[0]
    for e in es[1:]:
        den = den + e
    ti = jnp.zeros((rows, LANES), jnp.int32)
    tw = jnp.zeros((rows, LANES), F32)
    for k in range(TOP_K):
        ti = jnp.where(lane == k, idxs[k], ti)
        tw = jnp.where(lane == k, es[k] / den, tw)
    ti_ref[...] = ti
    tw_ref[...] = tw


def _merge(oa, ob, oc, gl, x, g1, sc2, sh2, nf, wbr, wo, wr_pad, br_pad, n_exp, tm):
    r, d = x.shape
    nt, gq, _ = g1.shape
    rowspec = lambda w: pl.BlockSpec((tm, w), lambda i: (i, 0))
    full = lambda shp: pl.BlockSpec(shp, lambda i: tuple(0 for _ in shp))
    mod = pl.BlockSpec((1, gq, d), lambda i: (i, 0, 0))
    return pl.pallas_call(
        functools.partial(_merge_kernel, n_exp),
        grid=(nt,),
        in_specs=[rowspec(A_WIDTH), rowspec(RG_WIDTH), rowspec(C_V_WIDTH), rowspec(N_BRANCH * d), rowspec(d),
                  mod, mod, mod, full((1, d)), full(wbr.shape), full((d, d)), full((d, LANES)),
                  full((1, LANES))],
        out_specs=[rowspec(d), rowspec(d), rowspec(LANES), rowspec(LANES)],
        out_shape=[jax.ShapeDtypeStruct((r, d), F32), jax.ShapeDtypeStruct((r, d), BF16),
                   jax.ShapeDtypeStruct((r, LANES), jnp.int32), jax.ShapeDtypeStruct((r, LANES), F32)],
        compiler_params=_cparams("arbitrary"),
        name="merge",
    )(oa, ob, oc, gl, x, g1, sc2, sh2, nf.reshape(1, d), wbr, wo, wr_pad, br_pad)


def _expert_kernel(te_ref, x_ref, w_ref, wg_ref, bg_ref, wu_ref, bu_ref, wd_ref, bd_ref, y_ref):
    x = x_ref[...]
    w = w_ref[...]
    g = _dot(x, wg_ref[0]) + bg_ref[0]
    u = _dot(x, wu_ref[0]) + bu_ref[0]
    g = jnp.minimum(g, SWIGLU_LIMIT)
    u = jnp.clip(u, -SWIGLU_LIMIT, SWIGLU_LIMIT)
    hid = (u + 1.0) * g * jax.nn.sigmoid(SWIGLU_ALPHA * g) * w
    y_ref[...] = _dot(hid.astype(BF16), wd_ref[0]) + w * bd_ref[0]


def _experts(tile_expert, xs, ws, wg, bg, wu, bu, wd, bd, tm):
    p, d = xs.shape
    n_exp, _, f = wg.shape
    grid_spec = pltpu.PrefetchScalarGridSpec(
        num_scalar_prefetch=1,
        grid=(p // tm,),
        in_specs=[pl.BlockSpec((tm, d), lambda i, te: (i, 0)),
                  pl.BlockSpec((tm, 1), lambda i, te: (i, 0)),
                  pl.BlockSpec((1, d, f), lambda i, te: (te[i], 0, 0)),
                  pl.BlockSpec((1, 1, f), lambda i, te: (te[i], 0, 0)),
                  pl.BlockSpec((1, d, f), lambda i, te: (te[i], 0, 0)),
                  pl.BlockSpec((1, 1, f), lambda i, te: (te[i], 0, 0)),
                  pl.BlockSpec((1, f, d), lambda i, te: (te[i], 0, 0)),
                  pl.BlockSpec((1, 1, d), lambda i, te: (te[i], 0, 0))],
        out_specs=pl.BlockSpec((tm, d), lambda i, te: (i, 0)),
    )
    return pl.pallas_call(
        _expert_kernel,
        grid_spec=grid_spec,
        out_shape=jax.ShapeDtypeStruct((p, d), F32),
        compiler_params=_cparams("arbitrary"),
        name="experts",
    )(tile_expert, xs, ws, wg, bg.reshape(n_exp, 1, f), wu, bu.reshape(n_exp, 1, f), wd, bd.reshape(n_exp, 1, d))


def _dispatch_plan(top_i, top_w, n_exp, tm):
    t_all = top_i.shape[0]
    a = t_all * TOP_K
    e = top_i.reshape(a)
    w = top_w.reshape(a)
    n_tiles = -(-a // tm) + n_exp
    onehot = (e[:, None] == jnp.arange(n_exp, dtype=jnp.int32)[None, :]).astype(jnp.int32)
    csum = jnp.cumsum(onehot, axis=0)
    counts = csum[-1]
    rank = jnp.take_along_axis(csum, e[:, None], axis=1)[:, 0] - 1
    padded = ((counts + tm - 1) // tm) * tm
    g_end = jnp.cumsum(padded)
    g_start = g_end - padded
    c_start = jnp.cumsum(counts) - counts
    pos = (g_start[e] + rank).reshape(t_all, TOP_K)
    order = jnp.argsort(e, stable=True).astype(jnp.int32)
    tile_start = jnp.arange(n_tiles, dtype=jnp.int32) * tm
    tile_expert = jnp.minimum(jnp.searchsorted(g_end, tile_start, side="right"), n_exp - 1).astype(jnp.int32)
    slot = jnp.arange(n_tiles * tm, dtype=jnp.int32)
    se = jnp.repeat(tile_expert, tm)
    off = slot - g_start[se]
    valid = (off < counts[se]) & (slot < g_end[n_exp - 1])
    src = order[jnp.clip(c_start[se] + off, 0, a - 1)]
    slot_token = jnp.where(valid, src // TOP_K, 0)
    slot_w = jnp.where(valid, w[src], 0.0)
    return tile_expert, slot_token, slot_w.reshape(-1, 1), pos


def _combine_kernel(final, yg_ref, x_ref, g2_ref, gn_ref, sc_ref, sh_ref, *outs):
    rows = x_ref.shape[0]
    moe = yg_ref[0]
    for k in range(1, TOP_K):
        moe = moe + yg_ref[k]
    x = x_ref[...] + _expand_groups(g2_ref[0], rows) * moe
    y = _rms(x) * gn_ref[...]
    if final:
        outs[0][...] = y
    else:
        outs[0][...] = x
        y = y * (1.0 + _expand_groups(sc_ref[0], rows)) + _expand_groups(sh_ref[0], rows)
        outs[1][...] = y.astype(BF16)


def _combine(yg, x, g2, gn, sc, sh, final, tm):
    r, d = x.shape
    nt, gq, _ = g2.shape
    rowspec = pl.BlockSpec((tm, d), lambda i: (i, 0))
    mod = pl.BlockSpec((1, gq, d), lambda i: (i, 0, 0))
    if final:
        out_specs, out_shape = [rowspec], [jax.ShapeDtypeStruct((r, d), F32)]
    else:
        out_specs = [rowspec, rowspec]
        out_shape = [jax.ShapeDtypeStruct((r, d), F32), jax.ShapeDtypeStruct((r, d), BF16)]
    return pl.pallas_call(
        functools.partial(_combine_kernel, final),
        grid=(nt,),
        in_specs=[pl.BlockSpec((TOP_K, tm, d), lambda i: (0, i, 0)), rowspec, mod,
                  pl.BlockSpec((1, d), lambda i: (0, 0)), mod, mod],
        out_specs=out_specs,
        out_shape=out_shape,
        compiler_params=_cparams("arbitrary"),
        name="combine",
    )(yg, x, g2, gn.reshape(1, d), sc, sh)


def _reordered_in_proj(w_in_l, b_f_l, d):
    sizes = (A_WIDTH, A_WIDTH, A_WIDTH, A_HEADS, RG_WIDTH, RG_WIDTH, C_QK_WIDTH, C_QK_WIDTH, C_V_WIDTH,
             C_V_WIDTH, N_BRANCH * d)
    names = ("qa", "ka", "va", "fa", "xb", "gb", "qc", "kc", "vc", "gc", "gl")
    src, o = {}, 0
    for nm, sz in zip(names, sizes):
        src[nm] = w_in_l[:, o:o + sz]
        o += sz
    src["qa"] = src["qa"] * (A_HEAD_DIM ** -0.5)
    src["fa"] = jnp.pad(src["fa"], ((0, 0), (0, LANES - A_HEADS)))
    cols, offs, o = [], {}, 0
    for nm, wdt in _IN_LAYOUT:
        wdt = src[nm].shape[1]
        cols.append(src[nm])
        offs[nm] = (o, o + wdt)
        o += wdt
    w_r = jnp.concatenate(cols, axis=1).astype(BF16)
    bf_pad = jnp.pad(b_f_l, (0, LANES - A_HEADS)).reshape(1, LANES)
    return w_r, bf_pad, offs


def _block_diag(w):
    nb, bi, bj = w.shape
    eye = jnp.eye(nb, dtype=w.dtype)
    return (eye[:, None, :, None] * w[:, :, None, :]).reshape(nb * bi, nb * bj)


def _tile_mods(m, rows_per_seq, tm):
    n_seq, d = m.shape
    if rows_per_seq >= tm:
        return jnp.repeat(m, rows_per_seq // tm, axis=0).reshape(-1, 1, d)
    return m.reshape(-1, tm // rows_per_seq, d)


def kernel(x_prompt, x_sample, c_prompt, c_sample, cache_k, cache_v, cache_logf, state_conv, state_h, state_ret, page_table, w_ada, b_ada, norm_mix, norm_ffn, w_in, b_f, conv_w, conv_b, w_rg_a, b_rg_a, w_rg_i, b_rg_i, lru_lambda, w_br, w_o, router_w, router_b, w_gate_e, b_gate_e, w_up_e, b_up_e, w_down_e, b_down_e, final_norm):
    n_p, seq, d = x_prompt.shape
    n_s, dec_seq, _ = x_sample.shape
    depth = w_ada.shape[0]
    n_pool, page = cache_k.shape[1], cache_k.shape[2]
    past_len = page_table.shape[1] * page
    n_exp = router_w.shape[-1]
    rp, rs = n_p * seq, n_s * dec_seq
    tm_p = min(ROW_TILE, rp)
    tm_s = min(ROW_TILE, rs)
    assert seq % tm_p == 0 and tm_p % SUBLANES == 0 and rs % tm_s == 0 and tm_s % dec_seq == 0
    assert dec_seq == SUBLANES and page == LANES

    n_c = n_p + n_s
    c_rows = -(-n_c // SUBLANES) * SUBLANES
    c_all = jnp.concatenate([c_prompt, c_sample, jnp.zeros((c_rows - n_c, d), F32)], axis=0)
    mod = _adaln(c_all, w_ada.astype(BF16), b_ada)

    def mods(layer, path):
        m = mod[layer, :n_p] if path == 0 else mod[layer, n_p:n_c]
        parts = jnp.split(m, 6, axis=-1)
        if path == 0:
            return [_tile_mods(p_, seq, tm_p) for p_ in parts]
        return [_tile_mods(p_, dec_seq, tm_s) for p_ in parts]

    xs_path = [x_prompt.reshape(rp, d), x_sample.reshape(rs, d)]
    geo = [(n_p, seq, tm_p, 0), (n_s, dec_seq, tm_s, past_len)]
    zero_state = [jnp.zeros((n_p, CONV_WIDTH - 1, RG_WIDTH), F32), jnp.zeros((n_p, RG_WIDTH), F32),
                  jnp.zeros((n_p, C_HEADS, C_QK_DIM, C_V_DIM), F32)]
    wr_pad = jnp.pad(router_w, ((0, 0), (0, 0), (0, LANES - n_exp))).astype(BF16)
    br_pad = jnp.pad(router_b, ((0, 0), (0, LANES - n_exp)))

    hn_path = [None, None]
    states = [[], []]
    for layer in range(depth):
        w_r, bf_pad, offs = _reordered_in_proj(w_in[layer], b_f[layer], d)
        wa_bd = _block_diag(w_rg_a[layer]).astype(BF16)
        wi_bd = _block_diag(w_rg_i[layer]).astype(BF16)
        wbr = w_br[layer].astype(BF16)
        wo = w_o[layer].astype(BF16)
        mid = [None, None]
        for path in (0, 1):
            n, l, tm, pos0 = geo[path]
            sh1, sc1, g1, sh2, sc2, g2 = mods(layer, path)
            if layer == 0:
                hn_path[path] = _norm_mod(xs_path[path], norm_mix[0], sc1, sh1, tm)
            outs = _in_proj(hn_path[path], w_r, bf_pad, offs, path == 0, tm)
            qrow, kf, vf, xb, gb, qc, kc, vc, gc, gl, lf = outs[:11]
            logf = lf[:, :A_HEADS].reshape(n, l, A_HEADS)
            lf_t = jnp.transpose(logf, (0, 2, 1))
            if path == 0:
                cum = _cumsum_lanes(lf_t.reshape(n * A_HEADS, l)).reshape(n, A_HEADS, l)
                oa = _fox_prompt(outs[11], outs[12], outs[13], cum, n, l)
                conv0, h0, s0 = zero_state
            else:
                lfn_t = jnp.pad(lf_t, ((0, 0), (0, 0), (0, page - l)))
                ck = cache_k[layer].reshape(n_pool, page, A_WIDTH)
                cv = cache_v[layer].reshape(n_pool, page, A_WIDTH)
                clf = jnp.transpose(cache_logf[layer], (0, 2, 1))
                oa = _fox_sample(qrow.reshape(n, l, A_WIDTH), kf.reshape(n, l, A_WIDTH),
                                 vf.reshape(n, l, A_WIDTH), lfn_t, ck, cv, clf, page_table).reshape(n * l, A_WIDTH)
                conv0, h0, s0 = state_conv[layer], state_h[layer], state_ret[layer]
            ob, conv_new, h_new = _rglru(xb, gb, conv0, h0, conv_w[layer], conv_b[layer], wa_bd, b_rg_a[layer],
                                         wi_bd, b_rg_i[layer], lru_lambda[layer], n, l)
            oc, s_new = _retention(qc, kc, vc, gc, s0, n, l, pos0)
            x_mid, hn2, ti, tw = _merge(oa, ob, oc, gl, xs_path[path], g1, sc2, sh2, norm_ffn[layer], wbr, wo,
                                        wr_pad[layer], br_pad[layer].reshape(1, LANES), n_exp, tm)
            mid[path] = (x_mid, hn2, ti[:, :TOP_K], tw[:, :TOP_K], g2)
            states[path].append((kf.reshape(n, l, A_HEADS, A_HEAD_DIM), vf.reshape(n, l, A_HEADS, A_HEAD_DIM),
                                 logf, conv_new, h_new.reshape(n, RG_WIDTH), s_new))

        hn2_all = jnp.concatenate([mid[0][1], mid[1][1]], axis=0)
        top_i = jnp.concatenate([mid[0][2], mid[1][2]], axis=0)
        top_w = jnp.concatenate([mid[0][3], mid[1][3]], axis=0)
        tile_expert, slot_token, slot_w, pos = _dispatch_plan(top_i, top_w, n_exp, ROW_TILE)
        xs_sorted = jnp.take(hn2_all, slot_token, axis=0)
        y = _experts(tile_expert, xs_sorted, slot_w, w_gate_e[layer].astype(BF16), b_gate_e[layer],
                     w_up_e[layer].astype(BF16), b_up_e[layer], w_down_e[layer].astype(BF16), b_down_e[layer],
                     ROW_TILE)
        yg = jnp.take(y, pos.T, axis=0)
        final = layer == depth - 1
        for path in (0, 1):
            n, l, tm, _ = geo[path]
            x_mid, _, _, _, g2 = mid[path]
            yg_p = yg[:, :rp] if path == 0 else yg[:, rp:]
            if final:
                gn, sc, sh = final_norm, g2, g2
            else:
                nxt = mods(layer + 1, path)
                gn, sc, sh = norm_mix[layer + 1], nxt[1], nxt[0]
            res = _combine(yg_p, x_mid, g2, gn, sc, sh, final, tm)
            if final:
                xs_path[path] = res[0]
            else:
                xs_path[path], hn_path[path] = res

    y_prompt = xs_path[0].reshape(n_p, seq, d)
    y_sample = xs_path[1].reshape(n_s, dec_seq, d)
    out = [y_prompt, y_sample]
    for path in (0, 1):
        for i in range(6):
            out.append(jnp.stack([s[i] for s in states[path]]))
    return tuple(out)
```

```python
import functools
import math

import numpy as np
import jax
import jax.numpy as jnp
from jax import lax
from jax.experimental import pallas as pl
from jax.experimental.pallas import tpu as pltpu

F32 = jnp.float32
BF16 = jnp.bfloat16
U32 = jnp.uint32

EPS = 1e-6
A_HEADS = 8
A_HEAD_DIM = 64
A_WIDTH = A_HEADS * A_HEAD_DIM
RG_WIDTH = 512
CONV_WIDTH = 4
LRU_C = 8.0
C_HEADS = 4
C_QK_DIM = 64
C_V_DIM = 128
C_QK_WIDTH = C_HEADS * C_QK_DIM
C_V_WIDTH = C_HEADS * C_V_DIM
RET_CHUNK = 128
ROPE_BASE = 10000.0
N_BRANCH = 3
TOP_K = 4
SWIGLU_LIMIT = 7.0
SWIGLU_ALPHA = 1.702

LANES = 128
SUBLANES = 8
BF16_ROWS = 16
VMEM_LIMIT = 56 * 1024 * 1024

ROW_TILE = 256
FLASH_TILE = 256
SCAN_TILE = 256
PAGES_PER_STEP = 16
HEADS_PER_STEP = 2

AUG_DIM = LANES
BIAS_LANE = A_HEAD_DIM
N_SPLIT = 3


def _cparams(*sem):
    return pltpu.CompilerParams(dimension_semantics=sem, vmem_limit_bytes=VMEM_LIMIT)


def _dot(a, b):
    return jnp.dot(a, b, preferred_element_type=F32)


def _dot_nt(a, b):
    return lax.dot_general(a, b, (((1,), (1,)), ((), ())), preferred_element_type=F32)


def _dot_tn(a, b):
    return lax.dot_general(a, b, (((0,), (0,)), ((), ())), preferred_element_type=F32)


def _log_sigmoid(x):
    return jnp.minimum(x, 0.0) - jnp.log1p(jnp.exp(-jnp.abs(x)))


def _softplus(x):
    return jnp.maximum(x, 0.0) + jnp.log1p(jnp.exp(-jnp.abs(x)))


def _rms(x):
    return x * lax.rsqrt(jnp.mean(x * x, axis=-1, keepdims=True) + EPS)


def _expand_groups(m, rows):
    g, d = m.shape
    if g == 1:
        return m
    return jnp.broadcast_to(m[:, None, :], (g, rows // g, d)).reshape(rows, d)


def _cumsum(c, axis, reverse=False):
    n = c.shape[axis]
    idx = lax.broadcasted_iota(jnp.int32, c.shape, axis)
    s = 1
    while s < n:
        if reverse:
            c = c + jnp.where(idx < n - s, pltpu.roll(c, n - s, axis=axis), 0.0)
        else:
            c = c + jnp.where(idx >= s, pltpu.roll(c, s, axis=axis), 0.0)
        s *= 2
    return c


def _split3(c):
    hi = c.astype(BF16).astype(F32)
    r1 = c - hi
    mid = r1.astype(BF16).astype(F32)
    lo = (r1 - mid).astype(BF16).astype(F32)
    return hi, mid, lo


def _adaln_kernel(c_ref, w_ref, b_ref, o_ref):
    c = c_ref[...]
    s = (c * jax.nn.sigmoid(c)).astype(BF16)
    o_ref[0] = _dot(s, w_ref[0]) + b_ref[0]


def _adaln(c_all, w_ada, b_ada):
    depth, d, six_d = w_ada.shape
    r = c_all.shape[0]
    nj = six_d // d
    return pl.pallas_call(
        _adaln_kernel,
        grid=(depth, nj),
        in_specs=[pl.BlockSpec((r, d), lambda l, j: (0, 0)),
                  pl.BlockSpec((1, d, d), lambda l, j: (l, 0, j)),
                  pl.BlockSpec((1, 1, d), lambda l, j: (l, 0, j))],
        out_specs=pl.BlockSpec((1, r, d), lambda l, j: (l, 0, j)),
        out_shape=jax.ShapeDtypeStruct((depth, r, six_d), F32),
        compiler_params=_cparams("arbitrary", "arbitrary"),
        name="adaln",
    )(c_all, w_ada, b_ada.reshape(depth, 1, six_d))


def _norm_mod_kernel(x_ref, g_ref, sc_ref, sh_ref, o_ref):
    x = x_ref[...]
    rows = x.shape[0]
    y = _rms(x) * g_ref[...]
    y = y * (1.0 + _expand_groups(sc_ref[0], rows)) + _expand_groups(sh_ref[0], rows)
    o_ref[...] = y.astype(BF16)


def _norm_mod(x, g, sc, sh, tm):
    r, d = x.shape
    nt, gq, _ = sc.shape
    return pl.pallas_call(
        _norm_mod_kernel,
        grid=(nt,),
        in_specs=[pl.BlockSpec((tm, d), lambda i: (i, 0)),
                  pl.BlockSpec((1, d), lambda i: (0, 0)),
                  pl.BlockSpec((1, gq, d), lambda i: (i, 0, 0)),
                  pl.BlockSpec((1, gq, d), lambda i: (i, 0, 0))],
        out_specs=pl.BlockSpec((tm, d), lambda i: (i, 0)),
        out_shape=jax.ShapeDtypeStruct((r, d), BF16),
        compiler_params=_cparams("arbitrary"),
        name="norm_mod",
    )(x, g.reshape(1, d), sc, sh)


_ROW_OUTS = (("xb", RG_WIDTH, F32), ("gb", RG_WIDTH, F32), ("qc", C_QK_WIDTH, F32), ("kc", C_QK_WIDTH, F32),
             ("vc", C_V_WIDTH, BF16), ("gc", C_V_WIDTH, F32), ("gl", None, F32))


def _write_row_outs(x, wnn_ref, offs_nn, refs):
    for (name, _, dt), ref in zip(_ROW_OUTS, refs):
        a, b = offs_nn[name]
        ref[...] = _dot(x, wnn_ref[:, a:b]).astype(dt)


def _in_proj_prompt_kernel(offs_nn, offs_nt, x_ref, wnn_ref, wnt_ref, bfr_ref, bfc_ref, esel_ref,
                           qt_ref, ka_ref, vtb_ref, ktf_ref, vtf_ref, lft_ref, *rest):
    row_refs, (carry_c, carry_r) = rest[:len(_ROW_OUTS)], rest[len(_ROW_OUTS):]
    i = pl.program_id(1)
    x = x_ref[...]
    tm = x.shape[0]

    @pl.when(i == 0)
    def _():
        carry_c[...] = jnp.zeros(carry_c.shape, F32)
        carry_r[...] = jnp.zeros(carry_r.shape, F32)

    def nn(name):
        a, b = offs_nn[name]
        return _dot(x, wnn_ref[:, a:b])

    def nt(name):
        a, b = offs_nt[name]
        return _dot_nt(wnt_ref[a:b, :], x)

    lf_c = _log_sigmoid(nn("fa") + bfr_ref[...])
    cum_c = _cumsum(lf_c, 0) + carry_c[...]
    carry_c[...] = cum_c[tm - 1:tm, :]
    lf_r = _log_sigmoid(nt("fa") + bfc_ref[...])
    cum_r = _cumsum(lf_r, 1) + carry_r[...]
    carry_r[...] = cum_r[:, tm - 1:tm]
    lft_ref[0] = lf_r[:A_HEADS, :]

    cs = jnp.concatenate([p.astype(BF16) for p in _split3(cum_c)], axis=1)
    hi_r, mid_r, lo_r = _split3(cum_r)
    kp = nn("kpad")
    lane = lax.broadcasted_iota(jnp.int32, (tm, AUG_DIM), 1)
    ones_k = jnp.where(lane >= BIAS_LANE, jnp.where(lane < BIAS_LANE + N_SPLIT, 1.0, 0.0), 0.0)
    rowi = lax.broadcasted_iota(jnp.int32, (BF16_ROWS, tm), 0)
    pad_rows = jnp.zeros((AUG_DIM - A_HEAD_DIM - BF16_ROWS, tm), BF16)
    q_t = nt("qa")
    k_t = nt("ka")
    v_t = nt("va")
    for h in range(A_HEADS):
        hs = slice(h * A_HEAD_DIM, (h + 1) * A_HEAD_DIM)
        ka_ref[0, h] = (kp[:, h * LANES:(h + 1) * LANES] + _dot(cs, esel_ref[h]) + ones_k).astype(BF16)
        bias = jnp.where(rowi == 0, hi_r[h:h + 1, :],
                         jnp.where(rowi == 1, mid_r[h:h + 1, :],
                                   jnp.where(rowi == 2, lo_r[h:h + 1, :],
                                             jnp.where(rowi < 2 * N_SPLIT, 1.0, 0.0))))
        qt_ref[0, h] = jnp.concatenate([q_t[hs, :].astype(BF16), bias.astype(BF16), pad_rows], axis=0)
        vtb_ref[0, h] = v_t[hs, :].astype(BF16)
        ktf_ref[0, h] = k_t[hs, :]
        vtf_ref[0, h] = v_t[hs, :]
    _write_row_outs(x, wnn_ref, offs_nn, row_refs)


def _in_proj_sample_kernel(offs_nn, offs_nt, x_ref, wnn_ref, wnt_ref, bfr_ref,
                           q_ref, kf_ref, vf_ref, lf_ref, *row_refs):
    x = x_ref[...]

    def nt(name):
        a, b = offs_nt[name]
        return _dot_nt(x, wnt_ref[a:b, :])

    q_ref[...] = nt("qa").astype(BF16)
    kf_ref[...] = nt("ka")
    vf_ref[...] = nt("va")
    a, b = offs_nn["fa"]
    lf_ref[...] = _log_sigmoid(_dot(x, wnn_ref[:, a:b]) + bfr_ref[...])
    _write_row_outs(x, wnn_ref, offs_nn, row_refs)


def _resident(shape, nd):
    zeros = tuple(0 for _ in shape)
    if nd == 1:
        return pl.BlockSpec(shape, lambda i: zeros, pipeline_mode=pl.Buffered(1))
    return pl.BlockSpec(shape, lambda b, i: zeros, pipeline_mode=pl.Buffered(1))


def _row_out_specs(r, d, tm, index):
    specs, shapes = [], []
    for _, w, dt in _ROW_OUTS:
        w = N_BRANCH * d if w is None else w
        specs.append(pl.BlockSpec((tm, w), index))
        shapes.append(jax.ShapeDtypeStruct((r, w), dt))
    return specs, shapes


def _in_proj_prompt(hn, w_nn, w_nt, bf_row, bf_col, esel, offs_nn, offs_nt, n, s, tm):
    r, d = hn.shape
    nt_ = s // tm
    row_specs, row_shapes = _row_out_specs(r, d, tm, lambda b, i: (b * nt_ + i, 0))
    fm = lambda rows, dt: (pl.BlockSpec((1, A_HEADS, rows, tm), lambda b, i: (b, 0, 0, i)),
                           jax.ShapeDtypeStruct((n, A_HEADS, rows, s), dt))
    outs = [fm(AUG_DIM, BF16),
            (pl.BlockSpec((1, A_HEADS, tm, AUG_DIM), lambda b, i: (b, 0, i, 0)),
             jax.ShapeDtypeStruct((n, A_HEADS, s, AUG_DIM), BF16)),
            fm(A_HEAD_DIM, BF16), fm(A_HEAD_DIM, F32), fm(A_HEAD_DIM, F32),
            (pl.BlockSpec((1, A_HEADS, tm), lambda b, i: (b, 0, i)), jax.ShapeDtypeStruct((n, A_HEADS, s), F32))]
    return pl.pallas_call(
        functools.partial(_in_proj_prompt_kernel, offs_nn, offs_nt),
        grid=(n, nt_),
        in_specs=[pl.BlockSpec((tm, d), lambda b, i: (b * nt_ + i, 0)),
                  _resident(w_nn.shape, 2), _resident(w_nt.shape, 2), _resident(bf_row.shape, 2),
                  _resident(bf_col.shape, 2), _resident(esel.shape, 2)],
        out_specs=[o[0] for o in outs] + row_specs,
        out_shape=[o[1] for o in outs] + row_shapes,
        scratch_shapes=[pltpu.VMEM((1, LANES), F32), pltpu.VMEM((BF16_ROWS, 1), F32)],
        compiler_params=_cparams("arbitrary", "arbitrary"),
        name="in_proj_prompt",
    )(hn, w_nn, w_nt, bf_row, bf_col, esel)


def _in_proj_sample(hn, w_nn, w_nt, bf_row, offs_nn, offs_nt, tm):
    r, d = hn.shape
    row_specs, row_shapes = _row_out_specs(r, d, tm, lambda i: (i, 0))
    row = lambda w, dt: (pl.BlockSpec((tm, w), lambda i: (i, 0)), jax.ShapeDtypeStruct((r, w), dt))
    outs = [row(A_WIDTH, BF16), row(A_WIDTH, F32), row(A_WIDTH, F32), row(LANES, F32)]
    return pl.pallas_call(
        functools.partial(_in_proj_sample_kernel, offs_nn, offs_nt),
        grid=(r // tm,),
        in_specs=[pl.BlockSpec((tm, d), lambda i: (i, 0)),
                  _resident(w_nn.shape, 1), _resident(w_nt.shape, 1), _resident(bf_row.shape, 1)],
        out_specs=[o[0] for o in outs] + row_specs,
        out_shape=[o[1] for o in outs] + row_shapes,
        compiler_params=_cparams("arbitrary"),
        name="in_proj_sample",
    )(hn, w_nn, w_nt, bf_row)


def _flash_kernel(t, q_ref, k_ref, v_ref, o_ref):
    qi = pl.program_id(2)
    key = lax.broadcasted_iota(jnp.int32, (t, t), 0)
    qry = lax.broadcasted_iota(jnp.int32, (t, t), 1)

    def tile(j, kb, carry, diagonal):
        m, l, acc = carry
        start = pl.multiple_of(kb * t, t)
        s = _dot(k_ref[0, j, pl.ds(start, t), :], q_ref[0, j])
        if diagonal:
            s = jnp.where(key <= qry, s, -jnp.inf)
        m_new = jnp.maximum(m, jnp.max(s, axis=0, keepdims=True))
        p = jnp.exp(s - m_new)
        alpha = jnp.exp(m - m_new)
        l = alpha * l + jnp.sum(p, axis=0, keepdims=True)
        acc = alpha * acc + _dot(v_ref[0, j, :, pl.ds(start, t)], p.astype(BF16))
        return m_new, l, acc

    def body(kb, carries):
        return tuple(tile(j, kb, carries[j], False) for j in range(HEADS_PER_STEP))

    init = tuple((jnp.full((1, t), -jnp.inf, F32), jnp.zeros((1, t), F32), jnp.zeros((A_HEAD_DIM, t), F32))
                 for _ in range(HEADS_PER_STEP))
    carries = lax.fori_loop(0, qi, body, init)
    outs = []
    for j in range(HEADS_PER_STEP):
        _, l, acc = tile(j, qi, carries[j], True)
        outs.append(acc / l)
    o_ref[...] = jnp.concatenate(outs, axis=0).T.astype(BF16)


def _fox_prompt(qt_aug, k_aug, vt, n, s):
    t = min(FLASH_TILE, s)
    nq = s // t
    hp = HEADS_PER_STEP
    return pl.pallas_call(
        functools.partial(_flash_kernel, t),
        grid=(n, A_HEADS // hp, nq),
        in_specs=[pl.BlockSpec((1, hp, AUG_DIM, t), lambda b, h, i: (b, h, 0, i)),
                  pl.BlockSpec((1, hp, s, AUG_DIM), lambda b, h, i: (b, h, 0, 0)),
                  pl.BlockSpec((1, hp, A_HEAD_DIM, s), lambda b, h, i: (b, h, 0, 0))],
        out_specs=pl.BlockSpec((t, hp * A_HEAD_DIM), lambda b, h, i: (b * nq + i, h)),
        out_shape=jax.ShapeDtypeStruct((n * s, A_WIDTH), BF16),
        compiler_params=_cparams("arbitrary", "arbitrary", "arbitrary"),
        name="fox_prompt",
    )(qt_aug, k_aug, vt)


def _paged_kernel(pps, page, pt_ref, q_ref, kn_ref, vn_ref, lfn_ref, *rest):
    k_refs = rest[:pps]
    v_refs = rest[pps:2 * pps]
    lf_refs = rest[2 * pps:3 * pps]
    o_ref = rest[3 * pps]
    qbd_ref, cq_ref, negc_ref, m_ref, l_ref, acc_ref, run_ref = rest[3 * pps + 1:]
    j = pl.program_id(1)
    nj = pl.num_programs(1)
    l_new = q_ref.shape[1]
    rows = A_HEADS * l_new
    lane_w = lax.broadcasted_iota(jnp.int32, (l_new, A_WIDTH), 1)
    row_i = lax.broadcasted_iota(jnp.int32, (rows, page), 0)
    lane_i = lax.broadcasted_iota(jnp.int32, (rows, page), 1)
    tok_of_row = row_i % l_new

    def expand(x):
        return jnp.broadcast_to(x[:, None, :], (A_HEADS, l_new, page)).reshape(rows, page)

    @pl.when(j == 0)
    def _():
        q = q_ref[0].astype(F32)
        qbd_ref[...] = jnp.concatenate(
            [jnp.where((lane_w // A_HEAD_DIM) == h, q, 0.0) for h in range(A_HEADS)], axis=0).astype(BF16)
        c = lfn_ref[0]
        lane = lax.broadcasted_iota(jnp.int32, c.shape, 1)
        s = 1
        while s < l_new:
            c = c + jnp.where(lane >= s, pltpu.roll(c, s, axis=1), 0.0)
            s *= 2
        ce = expand(c)
        negc_ref[...] = -ce
        cq_ref[...] = jnp.sum(jnp.where(lane_i == tok_of_row, ce, 0.0), axis=-1, keepdims=True)
        m_ref[...] = jnp.full(m_ref.shape, -jnp.inf, F32)
        l_ref[...] = jnp.zeros(l_ref.shape, F32)
        acc_ref[...] = jnp.zeros(acc_ref.shape, F32)
        run_ref[...] = jnp.zeros(run_ref.shape, F32)

    qbd = qbd_ref[...]
    cq = cq_ref[...]

    def update(s, pv_fn):
        m = m_ref[...]
        m_new = jnp.maximum(m, jnp.max(s, axis=-1, keepdims=True))
        p = jnp.exp(s - m_new)
        alpha = jnp.exp(m - m_new)
        l_ref[...] = alpha * l_ref[...] + jnp.sum(p, axis=-1, keepdims=True)
        acc_ref[...] = alpha * acc_ref[...] + pv_fn(p.astype(BF16))
        m_ref[...] = m_new

    incl, lfs = [], []
    for i in range(pps):
        lf = lf_refs[i][0, 0]
        lfs.append(lf)
        incl.append(_cumsum(lf, 1, reverse=True))
    run = run_ref[...]
    later = [None] * pps
    for i in reversed(range(pps)):
        later[i] = run
        run = run + incl[i][:, 0:1]
    run_ref[...] = run
    scores = [_dot(qbd, k_refs[i][0, 0].astype(BF16)) + expand(incl[i] - lfs[i] + later[i]) for i in range(pps)]

    def pv_pages(p):
        out = None
        for i in range(pps):
            c = _dot_nt(p[:, i * page:(i + 1) * page], v_refs[i][0, 0].astype(BF16))
            out = c if out is None else out + c
        return out

    update(jnp.concatenate(scores, axis=1) + cq, pv_pages)

    @pl.when(j == nj - 1)
    def _():
        pad = jnp.zeros((page - l_new, A_WIDTH), BF16)
        kn = jnp.concatenate([kn_ref[0].astype(BF16), pad], axis=0)
        vn = jnp.concatenate([vn_ref[0].astype(BF16), pad], axis=0)
        s = _dot_nt(qbd, kn) + negc_ref[...] + cq
        s = jnp.where(lane_i <= tok_of_row, s, -jnp.inf)
        update(s, lambda p: _dot(p, vn))
        o = acc_ref[...] / l_ref[...]
        out = jnp.zeros((l_new, A_WIDTH), F32)
        for h in range(A_HEADS):
            sel = (lane_w // A_HEAD_DIM) == h
            out = out + jnp.where(sel, o[h * l_new:(h + 1) * l_new, :], 0.0)
        o_ref[0] = out.astype(BF16)


def _fox_sample(layer, q, k_new, v_new, lfn_t, cache_kt, cache_vt, cache_lft, page_table):
    b, l_new, _ = q.shape
    page = cache_kt.shape[3]
    n_pages = page_table.shape[1]
    pps = math.gcd(PAGES_PER_STEP, n_pages)
    nj = n_pages // pps
    rows = A_HEADS * l_new

    def page_map(i):
        return lambda bi, j, pt: (layer, pt[bi * n_pages + (nj - 1 - j) * pps + i], 0, 0)

    seq = lambda w: pl.BlockSpec((1, l_new, w), lambda bi, j, pt: (bi, 0, 0))
    in_specs = [seq(A_WIDTH), seq(A_WIDTH), seq(A_WIDTH),
                pl.BlockSpec((1, A_HEADS, page), lambda bi, j, pt: (bi, 0, 0))]
    in_specs += [pl.BlockSpec((1, 1, A_WIDTH, page), page_map(i)) for i in range(pps)]
    in_specs += [pl.BlockSpec((1, 1, A_WIDTH, page), page_map(i)) for i in range(pps)]
    in_specs += [pl.BlockSpec((1, 1, A_HEADS, page), page_map(i)) for i in range(pps)]
    grid_spec = pltpu.PrefetchScalarGridSpec(
        num_scalar_prefetch=1,
        grid=(b, nj),
        in_specs=in_specs,
        out_specs=pl.BlockSpec((1, l_new, A_WIDTH), lambda bi, j, pt: (bi, 0, 0)),
        scratch_shapes=[pltpu.VMEM((rows, A_WIDTH), BF16), pltpu.VMEM((rows, 1), F32),
                        pltpu.VMEM((rows, page), F32), pltpu.VMEM((rows, 1), F32),
                        pltpu.VMEM((rows, 1), F32), pltpu.VMEM((rows, A_WIDTH), F32),
                        pltpu.VMEM((A_HEADS, 1), F32)],
    )
    return pl.pallas_call(
        functools.partial(_paged_kernel, pps, page),
        grid_spec=grid_spec,
        out_shape=jax.ShapeDtypeStruct((b, l_new, A_WIDTH), BF16),
        compiler_params=_cparams("arbitrary", "arbitrary"),
        name="fox_sample",
    )(page_table.reshape(-1), q, k_new, v_new, lfn_t,
      *([cache_kt] * pps), *([cache_vt] * pps), *([cache_lft] * pps))


def _rglru_kernel(tl, xb_ref, gb_ref, cs_ref, h0_ref, cw_ref, cb_ref, wa_ref, ba_ref, wi_ref, bi_ref, lam_ref,
                  y_ref, cn_ref, hn_ref, xpad, hcar):
    i = pl.program_id(1)
    hist = CONV_WIDTH - 1
    base = SUBLANES

    @pl.when(i == 0)
    def _():
        xpad[base - hist:base, :] = cs_ref[0]
        hcar[...] = h0_ref[0]

    x = xb_ref[...]
    xpad[base:base + tl, :] = x
    xc = cb_ref[...] + sum(xpad[base - hist + j:base - hist + j + tl, :] * cw_ref[j:j + 1, :]
                           for j in range(CONV_WIDTH))
    tail = xpad[base + tl - hist:base + tl, :]
    xpad[base - hist:base, :] = tail
    cn_ref[0] = tail

    xcb = xc.astype(BF16)
    r = jax.nn.sigmoid(_dot(xcb, wa_ref[...]) + ba_ref[...])
    ig = jax.nn.sigmoid(_dot(xcb, wi_ref[...]) + bi_ref[...])
    log_a = -LRU_C * r * _softplus(-lam_ref[...])
    a = jnp.exp(log_a)
    x2 = 2.0 * log_a
    e2 = jnp.exp(x2)
    em1 = jnp.where(e2 == 1.0, x2, jnp.where(x2 < -1.0, e2 - 1.0, (e2 - 1.0) * x2 / jnp.log(e2)))
    u = jnp.sqrt(-em1) * (ig * xc)

    row = lax.broadcasted_iota(jnp.int32, (tl, RG_WIDTH), 0)
    aa, bb = a, u
    s = 1
    while s < tl:
        keep = row >= s
        a_sh = jnp.where(keep, pltpu.roll(aa, s, axis=0), 1.0)
        b_sh = jnp.where(keep, pltpu.roll(bb, s, axis=0), 0.0)
        bb = aa * b_sh + bb
        aa = aa * a_sh
        s *= 2
    h = aa * hcar[...] + bb
    hlast = h[tl - 1:tl, :]
    hcar[...] = hlast
    hn_ref[0] = hlast
    y_ref[...] = (h * jax.nn.gelu(gb_ref[...])).astype(BF16)


def _rglru(xb, gb, conv_state, h0, cw, cb, wa_bd, ba, wi_bd, bi, lam, n, l):
    tl = min(SCAN_TILE, l)
    nt = l // tl
    w = RG_WIDTH
    hist = CONV_WIDTH - 1
    full = lambda shp: pl.BlockSpec(shp, lambda b, i: tuple(0 for _ in shp))
    return pl.pallas_call(
        functools.partial(_rglru_kernel, tl),
        grid=(n, nt),
        in_specs=[pl.BlockSpec((tl, w), lambda b, i: (b * nt + i, 0)),
                  pl.BlockSpec((tl, w), lambda b, i: (b * nt + i, 0)),
                  pl.BlockSpec((1, hist, w), lambda b, i: (b, 0, 0)),
                  pl.BlockSpec((1, 1, w), lambda b, i: (b, 0, 0)),
                  full((CONV_WIDTH, w)), full((1, w)), full((w, w)), full((1, w)), full((w, w)), full((1, w)),
                  full((1, w))],
        out_specs=[pl.BlockSpec((tl, w), lambda b, i: (b * nt + i, 0)),
                   pl.BlockSpec((1, hist, w), lambda b, i: (b, 0, 0)),
                   pl.BlockSpec((1, 1, w), lambda b, i: (b, 0, 0))],
        out_shape=[jax.ShapeDtypeStruct((n * l, w), BF16),
                   jax.ShapeDtypeStruct((n, hist, w), F32),
                   jax.ShapeDtypeStruct((n, 1, w), F32)],
        scratch_shapes=[pltpu.VMEM((tl + SUBLANES, w), F32), pltpu.VMEM((1, w), F32)],
        compiler_params=_cparams("arbitrary", "arbitrary"),
        name="rglru",
    )(xb, gb, conv_state, h0.reshape(n, 1, w), cw, cb.reshape(1, w), wa_bd, ba.reshape(1, w), wi_bd,
      bi.reshape(1, w), lam.reshape(1, w))


def _retention_kernel(t, qc_ref, kc_ref, vc_ref, gc_ref, cos_ref, sin_ref, qdec_ref, kdec_ref, dmask_ref,
                      cdec_ref, s0_ref, y_ref, sn_ref, st):
    c = pl.program_id(1)
    nc = pl.num_programs(1)

    @pl.when(c == 0)
    def _():
        st[...] = s0_ref[0]

    lane = lax.broadcasted_iota(jnp.int32, (t, C_QK_WIDTH), 1)
    first_half = (lane % C_QK_DIM) < (C_QK_DIM // 2)
    cos = cos_ref[...]
    sin = sin_ref[...]
    half = C_QK_DIM // 2

    def rot(x):
        swapped = jnp.where(first_half, pltpu.roll(x, C_QK_WIDTH - half, axis=1), pltpu.roll(x, half, axis=1))
        return x * cos + swapped * sin

    qr = rot(qc_ref[...])
    kr = rot(kc_ref[...]) * (C_QK_DIM ** -0.5)
    qb = qr.astype(BF16)
    kb = kr.astype(BF16)
    qd = (qr * qdec_ref[...]).astype(BF16)
    kd = (kr * kdec_ref[...]).astype(BF16)
    v = vc_ref[...]
    outs = []
    for h in range(C_HEADS):
        qs = slice(h * C_QK_DIM, (h + 1) * C_QK_DIM)
        vh = v[:, h * C_V_DIM:(h + 1) * C_V_DIM]
        sc = _dot_nt(qb[:, qs], kb[:, qs]) * dmask_ref[h]
        sh = st[h]
        o = _dot(sc.astype(BF16), vh) + _dot(qd[:, qs], sh.astype(BF16))
        st[h] = cdec_ref[h] * sh + _dot_tn(kd[:, qs], vh)
        outs.append(_rms(o))
    g = gc_ref[...]
    y_ref[...] = ((g * jax.nn.sigmoid(g)) * jnp.concatenate(outs, axis=1)).astype(BF16)

    @pl.when(c == nc - 1)
    def _():
        sn_ref[0] = st[...]


def _retention_tables(l, pos0):
    t = RET_CHUNK if l % RET_CHUNK == 0 else l
    half = C_QK_DIM // 2
    inv = ROPE_BASE ** (-jnp.arange(half, dtype=F32) / half)
    ang = (pos0 + jnp.arange(l)).astype(F32)[:, None] * inv[None, :]
    cos = jnp.tile(jnp.cos(ang), (1, 2 * C_HEADS))
    sin = jnp.tile(jnp.concatenate([-jnp.sin(ang), jnp.sin(ang)], axis=1), (1, C_HEADS))
    log_g = jnp.log1p(-jnp.exp2(-5.0 - jnp.arange(C_HEADS, dtype=F32)))
    idx = jnp.arange(t, dtype=F32)
    diff = idx[:, None] - idx[None, :]
    dmask = jnp.where(diff >= 0, jnp.exp(log_g[:, None, None] * jnp.maximum(diff, 0.0)), 0.0)
    kdec = jnp.repeat(jnp.exp(log_g[None, :] * (t - 1.0 - idx)[:, None]), C_QK_DIM, axis=1)
    qdec = jnp.repeat(jnp.exp(log_g[None, :] * (idx[:, None] + 1.0)), C_QK_DIM, axis=1)
    cdec = jnp.broadcast_to(jnp.exp(log_g * t)[:, None, None], (C_HEADS, 1, C_V_DIM))
    return t, cos, sin, qdec, kdec, dmask, cdec


def _retention(qc, kc, vc, gc, s0, n, l, pos0):
    t, cos, sin, qdec, kdec, dmask, cdec = _retention_tables(l, pos0)
    nc = l // t
    full = lambda shp: pl.BlockSpec(shp, lambda b, c: tuple(0 for _ in shp))
    rowspec = lambda w: pl.BlockSpec((t, w), lambda b, c: (b * nc + c, 0))
    return pl.pallas_call(
        functools.partial(_retention_kernel, t),
        grid=(n, nc),
        in_specs=[rowspec(C_QK_WIDTH), rowspec(C_QK_WIDTH), rowspec(C_V_WIDTH), rowspec(C_V_WIDTH),
                  pl.BlockSpec((t, C_QK_WIDTH), lambda b, c: (c, 0)),
                  pl.BlockSpec((t, C_QK_WIDTH), lambda b, c: (c, 0)),
                  full((t, C_QK_WIDTH)), full((t, C_QK_WIDTH)), full((C_HEADS, t, t)),
                  full((C_HEADS, 1, C_V_DIM)),
                  pl.BlockSpec((1, C_HEADS, C_QK_DIM, C_V_DIM), lambda b, c: (b, 0, 0, 0))],
        out_specs=[rowspec(C_V_WIDTH),
                   pl.BlockSpec((1, C_HEADS, C_QK_DIM, C_V_DIM), lambda b, c: (b, 0, 0, 0))],
        out_shape=[jax.ShapeDtypeStruct((n * l, C_V_WIDTH), BF16),
                   jax.ShapeDtypeStruct((n, C_HEADS, C_QK_DIM, C_V_DIM), F32)],
        scratch_shapes=[pltpu.VMEM((C_HEADS, C_QK_DIM, C_V_DIM), F32)],
        compiler_params=_cparams("arbitrary", "arbitrary"),
        name="retention",
    )(qc, kc, vc, gc, cos, sin, qdec, kdec, dmask, cdec, s0)


def _merge_kernel(n_exp, oa_ref, ob_ref, oc_ref, gl_ref, x_ref, g1_ref, sc_ref, sh_ref, nf_ref, wbr_ref, wo_ref,
                  wr_ref, br_ref, xm_ref, hp_ref, ti_ref, tw_ref):
    rows, d = x_ref.shape
    m = jnp.zeros((rows, d), F32)
    for b, (o_ref, w0) in enumerate(((oa_ref, 0), (ob_ref, A_WIDTH), (oc_ref, A_WIDTH + RG_WIDTH))):
        width = o_ref.shape[1]
        gate = jax.nn.sigmoid(gl_ref[:, b * d:(b + 1) * d])
        m = m + gate * _dot(o_ref[...], wbr_ref[w0:w0 + width, :])
    x = x_ref[...] + _expand_groups(g1_ref[0], rows) * _dot(m.astype(BF16), wo_ref[...])
    xm_ref[...] = x
    hn = _rms(x) * nf_ref[...]
    hn = (hn * (1.0 + _expand_groups(sc_ref[0], rows)) + _expand_groups(sh_ref[0], rows)).astype(BF16)
    bits = pltpu.bitcast(hn.astype(F32), U32)
    hp_ref[...] = bits[:, :d // 2] | (bits[:, d // 2:] >> 16)
    lane = lax.broadcasted_iota(jnp.int32, (rows, LANES), 1)
    logits = jnp.where(lane < n_exp, _dot(hn, wr_ref[...]) + br_ref[...], -jnp.inf)
    vals, idxs = [], []
    for _ in range(TOP_K):
        mx = jnp.max(logits, axis=-1, keepdims=True)
        ix = jnp.min(jnp.where(logits == mx, lane, LANES), axis=-1, keepdims=True)
        vals.append(mx)
        idxs.append(ix)
        logits = jnp.where(lane == ix, -jnp.inf, logits)
    es = [jnp.exp(v - vals[0]) for v in vals]
    den = es[0]
    for e in es[1:]:
        den = den + e
    ti = jnp.zeros((rows, LANES), jnp.int32)
    tw = jnp.zeros((rows, LANES), F32)
    for k in range(TOP_K):
        ti = jnp.where(lane == k, idxs[k], ti)
        tw = jnp.where(lane == k, es[k] / den, tw)
    ti_ref[...] = ti
    tw_ref[...] = tw


def _merge(oa, ob, oc, gl, x, g1, sc2, sh2, nf, wbr, wo, wr_pad, br_pad, n_exp, tm):
    r, d = x.shape
    nt, gq, _ = g1.shape
    rowspec = lambda w: pl.BlockSpec((tm, w), lambda i: (i, 0))
    full = lambda shp: pl.BlockSpec(shp, lambda i: tuple(0 for _ in shp))
    mod = pl.BlockSpec((1, gq, d), lambda i: (i, 0, 0))
    return pl.pallas_call(
        functools.partial(_merge_kernel, n_exp),
        grid=(nt,),
        in_specs=[rowspec(A_WIDTH), rowspec(RG_WIDTH), rowspec(C_V_WIDTH), rowspec(N_BRANCH * d), rowspec(d),
                  mod, mod, mod, full((1, d)), full(wbr.shape), full((d, d)), full((d, LANES)),
                  full((1, LANES))],
        out_specs=[rowspec(d), rowspec(d // 2), rowspec(LANES), rowspec(LANES)],
        out_shape=[jax.ShapeDtypeStruct((r, d), F32), jax.ShapeDtypeStruct((r, d // 2), U32),
                   jax.ShapeDtypeStruct((r, LANES), jnp.int32), jax.ShapeDtypeStruct((r, LANES), F32)],
        compiler_params=_cparams("arbitrary"),
        name="merge",
    )(oa, ob, oc, gl, x, g1, sc2, sh2, nf.reshape(1, d), wbr, wo, wr_pad, br_pad)


def _expert_kernel(te_ref, x_ref, w_ref, wg_ref, bg_ref, wu_ref, bu_ref, wd_ref, bd_ref, y_ref):
    words = x_ref[...]
    x = jnp.concatenate([pltpu.bitcast(words & jnp.uint32(0xFFFF0000), F32).astype(BF16),
                         pltpu.bitcast(words << 16, F32).astype(BF16)], axis=1)
    w = w_ref[...]
    g = _dot(x, wg_ref[0]) + bg_ref[0]
    u = _dot(x, wu_ref[0]) + bu_ref[0]
    g = jnp.minimum(g, SWIGLU_LIMIT)
    u = jnp.clip(u, -SWIGLU_LIMIT, SWIGLU_LIMIT)
    hid = (u + 1.0) * g * jax.nn.sigmoid(SWIGLU_ALPHA * g) * w
    y_ref[...] = _dot(hid.astype(BF16), wd_ref[0]) + w * bd_ref[0]


def _experts(tile_expert, xs, ws, wg, bg, wu, bu, wd, bd, tm):
    p, half_d = xs.shape
    n_exp, d, f = wg.shape
    grid_spec = pltpu.PrefetchScalarGridSpec(
        num_scalar_prefetch=1,
        grid=(p // tm,),
        in_specs=[pl.BlockSpec((tm, half_d), lambda i, te: (i, 0)),
                  pl.BlockSpec((tm, 1), lambda i, te: (i, 0)),
                  pl.BlockSpec((1, d, f), lambda i, te: (te[i], 0, 0)),
                  pl.BlockSpec((1, 1, f), lambda i, te: (te[i], 0, 0)),
                  pl.BlockSpec((1, d, f), lambda i, te: (te[i], 0, 0)),
                  pl.BlockSpec((1, 1, f), lambda i, te: (te[i], 0, 0)),
                  pl.BlockSpec((1, f, d), lambda i, te: (te[i], 0, 0)),
                  pl.BlockSpec((1, 1, d), lambda i, te: (te[i], 0, 0))],
        out_specs=pl.BlockSpec((tm, d), lambda i, te: (i, 0)),
    )
    return pl.pallas_call(
        _expert_kernel,
        grid_spec=grid_spec,
        out_shape=jax.ShapeDtypeStruct((p, d), F32),
        compiler_params=_cparams("arbitrary"),
        name="experts",
    )(tile_expert, xs, ws, wg, bg.reshape(n_exp, 1, f), wu, bu.reshape(n_exp, 1, f), wd, bd.reshape(n_exp, 1, d))


def _dispatch_plan(top_i, top_w, n_exp, tm):
    t_all = top_i.shape[0]
    a = t_all * TOP_K
    assert a % LANES == 0
    e = top_i.reshape(a)
    w = top_w.reshape(a)
    n_tiles = -(-a // tm) + n_exp
    onehot = (e.reshape(-1, LANES, 1) == jnp.arange(n_exp, dtype=jnp.int32)).astype(BF16)
    tri = (jnp.arange(LANES)[:, None] >= jnp.arange(LANES)[None, :]).astype(BF16)
    within = jnp.einsum("ij,bjk->bik", tri, onehot, preferred_element_type=F32)
    blk_tot = within[:, -1, :]
    blk_end = jnp.cumsum(blk_tot, axis=0)
    counts = blk_end[-1].astype(jnp.int32)
    padded = ((counts + tm - 1) // tm) * tm
    g_end = jnp.cumsum(padded)
    g_start = g_end - padded
    c_start = jnp.cumsum(counts) - counts
    base = (blk_end - blk_tot)[:, None, :] + g_start.astype(F32)[None, None, :]
    pos = (jnp.sum(onehot.astype(F32) * (within + base), axis=-1) - 1.0).astype(jnp.int32).reshape(t_all, TOP_K)
    order = jnp.argsort(e, stable=True).astype(jnp.int32)
    tile_start = jnp.arange(n_tiles, dtype=jnp.int32) * tm
    tile_expert = jnp.minimum(jnp.searchsorted(g_end, tile_start, side="right"), n_exp - 1).astype(jnp.int32)
    off = (tile_start - g_start[tile_expert])[:, None] + jnp.arange(tm, dtype=jnp.int32)[None, :]
    valid = (off < counts[tile_expert][:, None]) & (tile_start < g_end[n_exp - 1])[:, None]
    src = jnp.take(order, jnp.clip(c_start[tile_expert][:, None] + off, 0, a - 1).reshape(-1), mode="clip")
    valid = valid.reshape(-1)
    slot_token = jnp.where(valid, src // TOP_K, 0)
    slot_w = jnp.where(valid, jnp.take(w, src, mode="clip"), 0.0)
    return tile_expert, slot_token, slot_w.reshape(-1, 1), pos


def _combine_kernel(final, yg_ref, x_ref, g2_ref, gn_ref, sc_ref, sh_ref, *outs):
    rows = x_ref.shape[0]
    moe = yg_ref[0]
    for k in range(1, TOP_K):
        moe = moe + yg_ref[k]
    x = x_ref[...] + _expand_groups(g2_ref[0], rows) * moe
    y = _rms(x) * gn_ref[...]
    if final:
        outs[0][...] = y
    else:
        outs[0][...] = x
        y = y * (1.0 + _expand_groups(sc_ref[0], rows)) + _expand_groups(sh_ref[0], rows)
        outs[1][...] = y.astype(BF16)


def _combine(yg, tile0, x, g2, gn, sc, sh, final, tm):
    r, d = x.shape
    nt, gq, _ = g2.shape
    rowspec = pl.BlockSpec((tm, d), lambda i: (i, 0))
    mod = pl.BlockSpec((1, gq, d), lambda i: (i, 0, 0))
    if final:
        out_specs, out_shape = [rowspec], [jax.ShapeDtypeStruct((r, d), F32)]
    else:
        out_specs = [rowspec, rowspec]
        out_shape = [jax.ShapeDtypeStruct((r, d), F32), jax.ShapeDtypeStruct((r, d), BF16)]
    return pl.pallas_call(
        functools.partial(_combine_kernel, final),
        grid=(nt,),
        in_specs=[pl.BlockSpec((TOP_K, tm, d), lambda i: (0, i + tile0, 0)), rowspec, mod,
                  pl.BlockSpec((1, d), lambda i: (0, 0)), mod, mod],
        out_specs=out_specs,
        out_shape=out_shape,
        compiler_params=_cparams("arbitrary"),
        name="combine",
    )(yg, x, g2, gn.reshape(1, d), sc, sh)


def _prep_in_proj(w_in_l, b_f_l, d):
    sizes = (A_WIDTH, A_WIDTH, A_WIDTH, A_HEADS, RG_WIDTH, RG_WIDTH, C_QK_WIDTH, C_QK_WIDTH, C_V_WIDTH,
             C_V_WIDTH, N_BRANCH * d)
    names = ("qa", "ka", "va", "fa", "xb", "gb", "qc", "kc", "vc", "gc", "gl")
    src, o = {}, 0
    for nm, sz in zip(names, sizes):
        src[nm] = w_in_l[:, o:o + sz]
        o += sz
    kpad = jnp.pad(src["ka"].reshape(d, A_HEADS, A_HEAD_DIM),
                   ((0, 0), (0, 0), (0, LANES - A_HEAD_DIM))).reshape(d, A_HEADS * LANES)
    fa = jnp.pad(src["fa"], ((0, 0), (0, LANES - A_HEADS)))
    nn_parts = [("kpad", kpad)] + [(nm, src[nm]) for nm, _, _ in _ROW_OUTS] + [("fa", fa)]
    nt_parts = [("qa", (src["qa"] * (A_HEAD_DIM ** -0.5)).T), ("ka", src["ka"].T), ("va", src["va"].T),
                ("fa", jnp.pad(src["fa"].T, ((0, BF16_ROWS - A_HEADS), (0, 0))))]

    def pack(parts, axis):
        offs, o = {}, 0
        for nm, p_ in parts:
            offs[nm] = (o, o + p_.shape[axis])
            o += p_.shape[axis]
        return jnp.concatenate([p_ for _, p_ in parts], axis=axis).astype(BF16), offs

    w_nn, offs_nn = pack(nn_parts, 1)
    w_nt, offs_nt = pack(nt_parts, 0)
    bf_row = jnp.pad(b_f_l, (0, LANES - A_HEADS)).reshape(1, LANES)
    bf_col = jnp.pad(b_f_l, (0, BF16_ROWS - A_HEADS)).reshape(BF16_ROWS, 1)
    return w_nn, w_nt, offs_nn, offs_nt, bf_row, bf_col


def _bias_select():
    e = np.zeros((A_HEADS, N_SPLIT * LANES, AUG_DIM), np.float32)
    for h in range(A_HEADS):
        for p_ in range(N_SPLIT):
            e[h, p_ * LANES + h, BIAS_LANE + N_SPLIT + p_] = -1.0
    return jnp.asarray(e, BF16)


def _block_diag(w):
    nb, bi, bj = w.shape
    eye = jnp.eye(nb, dtype=w.dtype)
    return (eye[:, None, :, None] * w[:, :, None, :]).reshape(nb * bi, nb * bj)


def _tile_mods(m, rows_per_seq, tm):
    n_seq, d = m.shape
    if rows_per_seq >= tm:
        return jnp.repeat(m, rows_per_seq // tm, axis=0).reshape(-1, 1, d)
    return m.reshape(-1, tm // rows_per_seq, d)


def kernel(x_prompt, x_sample, c_prompt, c_sample, cache_k, cache_v, cache_logf, state_conv, state_h, state_ret, page_table, w_ada, b_ada, norm_mix, norm_ffn, w_in, b_f, conv_w, conv_b, w_rg_a, b_rg_a, w_rg_i, b_rg_i, lru_lambda, w_br, w_o, router_w, router_b, w_gate_e, b_gate_e, w_up_e, b_up_e, w_down_e, b_down_e, final_norm):
    n_p, seq, d = x_prompt.shape
    n_s, dec_seq, _ = x_sample.shape
    depth = w_ada.shape[0]
    n_pool, page = cache_k.shape[1], cache_k.shape[2]
    past_len = page_table.shape[1] * page
    n_exp = router_w.shape[-1]
    rp, rs = n_p * seq, n_s * dec_seq
    tm_p = min(ROW_TILE, rp)
    tm_s = min(ROW_TILE, rs)
    assert seq % tm_p == 0 and tm_p % SUBLANES == 0 and rs % tm_s == 0 and tm_s % dec_seq == 0
    assert dec_seq == SUBLANES and page == LANES and rp % ROW_TILE == 0

    n_c = n_p + n_s
    c_rows = -(-n_c // SUBLANES) * SUBLANES
    c_all = jnp.concatenate([c_prompt, c_sample, jnp.zeros((c_rows - n_c, d), F32)], axis=0)
    mod = _adaln(c_all, w_ada.astype(BF16), b_ada)

    def mods(layer, path):
        m = mod[layer, :n_p] if path == 0 else mod[layer, n_p:n_c]
        parts = jnp.split(m, 6, axis=-1)
        if path == 0:
            return [_tile_mods(p_, seq, tm_p) for p_ in parts]
        return [_tile_mods(p_, dec_seq, tm_s) for p_ in parts]

    xs_path = [x_prompt.reshape(rp, d), x_sample.reshape(rs, d)]
    geo = [(n_p, seq, tm_p, 0), (n_s, dec_seq, tm_s, past_len)]
    zero_state = [jnp.zeros((n_p, CONV_WIDTH - 1, RG_WIDTH), F32), jnp.zeros((n_p, RG_WIDTH), F32),
                  jnp.zeros((n_p, C_HEADS, C_QK_DIM, C_V_DIM), F32)]
    wr_pad = jnp.pad(router_w, ((0, 0), (0, 0), (0, LANES - n_exp))).astype(BF16)
    br_pad = jnp.pad(router_b, ((0, 0), (0, LANES - n_exp)))
    esel = _bias_select()
    cache_kt = jnp.transpose(cache_k, (0, 1, 3, 4, 2)).reshape(depth, n_pool, A_WIDTH, page)
    cache_vt = jnp.transpose(cache_v, (0, 1, 3, 4, 2)).reshape(depth, n_pool, A_WIDTH, page)
    cache_lft = jnp.transpose(cache_logf, (0, 1, 3, 2))

    hn_path = [None, None]
    states = [[], []]
    for layer in range(depth):
        w_nn, w_nt, offs_nn, offs_nt, bf_row, bf_col = _prep_in_proj(w_in[layer], b_f[layer], d)
        wa_bd = _block_diag(w_rg_a[layer]).astype(BF16)
        wi_bd = _block_diag(w_rg_i[layer]).astype(BF16)
        wbr = w_br[layer].astype(BF16)
        wo = w_o[layer].astype(BF16)
        mid = [None, None]
        for path in (0, 1):
            n, l, tm, pos0 = geo[path]
            sh1, sc1, g1, sh2, sc2, g2 = mods(layer, path)
            if layer == 0:
                hn_path[path] = _norm_mod(xs_path[path], norm_mix[0], sc1, sh1, tm)
            if path == 0:
                outs = _in_proj_prompt(hn_path[0], w_nn, w_nt, bf_row, bf_col, esel, offs_nn, offs_nt, n, l, tm)
                qt_aug, k_aug, vt_b, kt_f, vt_f, lf_t = outs[:6]
                xb, gb, qc, kc, vc, gc, gl = outs[6:]
                oa = _fox_prompt(qt_aug, k_aug, vt_b, n, l)
                conv0, h0, s0 = zero_state
                st_k, st_v, st_f = kt_f, vt_f, lf_t
            else:
                outs = _in_proj_sample(hn_path[1], w_nn, w_nt, bf_row, offs_nn, offs_nt, tm)
                qrow, kf, vf, lf = outs[:4]
                xb, gb, qc, kc, vc, gc, gl = outs[4:]
                logf = lf[:, :A_HEADS].reshape(n, l, A_HEADS)
                lfn_t = jnp.pad(jnp.transpose(logf, (0, 2, 1)), ((0, 0), (0, 0), (0, page - l)))
                oa = _fox_sample(layer, qrow.reshape(n, l, A_WIDTH), kf.reshape(n, l, A_WIDTH),
                                 vf.reshape(n, l, A_WIDTH), lfn_t, cache_kt, cache_vt, cache_lft,
                                 page_table).reshape(n * l, A_WIDTH)
                conv0, h0, s0 = state_conv[layer], state_h[layer], state_ret[layer]
                st_k = kf.reshape(n, l, A_HEADS, A_HEAD_DIM)
                st_v = vf.reshape(n, l, A_HEADS, A_HEAD_DIM)
                st_f = logf
            ob, conv_new, h_new = _rglru(xb, gb, conv0, h0, conv_w[layer], conv_b[layer], wa_bd, b_rg_a[layer],
                                         wi_bd, b_rg_i[layer], lru_lambda[layer], n, l)
            oc, s_new = _retention(qc, kc, vc, gc, s0, n, l, pos0)
            x_mid, hn2, ti, tw = _merge(oa, ob, oc, gl, xs_path[path], g1, sc2, sh2, norm_ffn[layer], wbr, wo,
                                        wr_pad[layer], br_pad[layer].reshape(1, LANES), n_exp, tm)
            mid[path] = (x_mid, hn2, ti[:, :TOP_K], tw[:, :TOP_K], g2)
            states[path].append((st_k, st_v, st_f, conv_new, h_new.reshape(n, RG_WIDTH), s_new))

        hn2_all = jnp.concatenate([mid[0][1], mid[1][1]], axis=0)
        top_i = jnp.concatenate([mid[0][2], mid[1][2]], axis=0)
        top_w = jnp.concatenate([mid[0][3], mid[1][3]], axis=0)
        tile_expert, slot_token, slot_w, pos = _dispatch_plan(top_i, top_w, n_exp, ROW_TILE)
        xs_sorted = jnp.take(hn2_all, slot_token, axis=0, mode="clip")
        y = _experts(tile_expert, xs_sorted, slot_w, w_gate_e[layer].astype(BF16), b_gate_e[layer],
                     w_up_e[layer].astype(BF16), b_up_e[layer], w_down_e[layer].astype(BF16), b_down_e[layer],
                     ROW_TILE)
        yg = jnp.take(y, pos.T, axis=0, mode="clip")
        final = layer == depth - 1
        for path in (0, 1):
            n, l, tm, _ = geo[path]
            x_mid, _, _, _, g2 = mid[path]
            if final:
                gn, sc, sh = final_norm, g2, g2
            else:
                nxt = mods(layer + 1, path)
                gn, sc, sh = norm_mix[layer + 1], nxt[1], nxt[0]
            res = _combine(yg, 0 if path == 0 else rp // tm, x_mid, g2, gn, sc, sh, final, tm)
            if final:
                xs_path[path] = res[0]
            else:
                xs_path[path], hn_path[path] = res

    y_prompt = xs_path[0].reshape(n_p, seq, d)
    y_sample = xs_path[1].reshape(n_s, dec_seq, d)
    out = [y_prompt, y_sample]
    stack = lambda path, i: jnp.stack([s[i] for s in states[path]])
    out.append(jnp.transpose(stack(0, 0), (0, 1, 4, 2, 3)))
    out.append(jnp.transpose(stack(0, 1), (0, 1, 4, 2, 3)))
    out.append(jnp.transpose(stack(0, 2), (0, 1, 3, 2)))
    out += [stack(0, i) for i in range(3, 6)]
    out += [stack(1, i) for i in range(6)]
    return tuple(out)
```

```python
import functools
import math

import numpy as np
import jax
import jax.numpy as jnp
from jax import lax
from jax.experimental import pallas as pl
from jax.experimental.pallas import tpu as pltpu

F32 = jnp.float32
BF16 = jnp.bfloat16
U32 = jnp.uint32

EPS = 1e-6
A_HEADS = 8
A_HEAD_DIM = 64
A_WIDTH = A_HEADS * A_HEAD_DIM
RG_WIDTH = 512
CONV_WIDTH = 4
LRU_C = 8.0
C_HEADS = 4
C_QK_DIM = 64
C_V_DIM = 128
C_QK_WIDTH = C_HEADS * C_QK_DIM
C_V_WIDTH = C_HEADS * C_V_DIM
RET_CHUNK = 128
ROPE_BASE = 10000.0
N_BRANCH = 3
TOP_K = 4
SWIGLU_LIMIT = 7.0
SWIGLU_ALPHA = 1.702

LANES = 128
SUBLANES = 8
BF16_ROWS = 16
VMEM_LIMIT = 56 * 1024 * 1024

ROW_TILE = 256
FLASH_TILE = 256
SCAN_TILE = 256
PAGES_PER_STEP = 16

AUG_DIM = LANES
BIAS_LANE = A_HEAD_DIM
N_SPLIT = 3


def _cparams(*sem):
    return pltpu.CompilerParams(dimension_semantics=sem, vmem_limit_bytes=VMEM_LIMIT)


def _dot(a, b):
    return jnp.dot(a, b, preferred_element_type=F32)


def _dot_nt(a, b):
    return lax.dot_general(a, b, (((1,), (1,)), ((), ())), preferred_element_type=F32)


def _dot_tn(a, b):
    return lax.dot_general(a, b, (((0,), (0,)), ((), ())), preferred_element_type=F32)


def _log_sigmoid(x):
    return jnp.minimum(x, 0.0) - jnp.log1p(jnp.exp(-jnp.abs(x)))


def _softplus(x):
    return jnp.maximum(x, 0.0) + jnp.log1p(jnp.exp(-jnp.abs(x)))


def _rms(x):
    return x * lax.rsqrt(jnp.mean(x * x, axis=-1, keepdims=True) + EPS)


def _expand_groups(m, rows):
    g, d = m.shape
    if g == 1:
        return m
    return jnp.broadcast_to(m[:, None, :], (g, rows // g, d)).reshape(rows, d)


def _cumsum(c, axis, reverse=False):
    n = c.shape[axis]
    idx = lax.broadcasted_iota(jnp.int32, c.shape, axis)
    s = 1
    while s < n:
        if reverse:
            c = c + jnp.where(idx < n - s, pltpu.roll(c, n - s, axis=axis), 0.0)
        else:
            c = c + jnp.where(idx >= s, pltpu.roll(c, s, axis=axis), 0.0)
        s *= 2
    return c


def _split3(c):
    hi = c.astype(BF16).astype(F32)
    r1 = c - hi
    mid = r1.astype(BF16).astype(F32)
    lo = (r1 - mid).astype(BF16).astype(F32)
    return hi, mid, lo


def _adaln_kernel(c_ref, w_ref, b_ref, o_ref):
    c = c_ref[...]
    s = (c * jax.nn.sigmoid(c)).astype(BF16)
    o_ref[0] = _dot(s, w_ref[0]) + b_ref[0]


def _adaln(c_all, w_ada, b_ada):
    depth, d, six_d = w_ada.shape
    r = c_all.shape[0]
    nj = six_d // d
    return pl.pallas_call(
        _adaln_kernel,
        grid=(depth, nj),
        in_specs=[pl.BlockSpec((r, d), lambda l, j: (0, 0)),
                  pl.BlockSpec((1, d, d), lambda l, j: (l, 0, j)),
                  pl.BlockSpec((1, 1, d), lambda l, j: (l, 0, j))],
        out_specs=pl.BlockSpec((1, r, d), lambda l, j: (l, 0, j)),
        out_shape=jax.ShapeDtypeStruct((depth, r, six_d), F32),
        compiler_params=_cparams("arbitrary", "arbitrary"),
        name="adaln",
    )(c_all, w_ada, b_ada.reshape(depth, 1, six_d))


def _norm_mod_kernel(x_ref, g_ref, sc_ref, sh_ref, o_ref):
    x = x_ref[...]
    rows = x.shape[0]
    y = _rms(x) * g_ref[...]
    y = y * (1.0 + _expand_groups(sc_ref[0], rows)) + _expand_groups(sh_ref[0], rows)
    o_ref[...] = y.astype(BF16)


def _norm_mod(x, g, sc, sh, tm):
    r, d = x.shape
    nt, gq, _ = sc.shape
    return pl.pallas_call(
        _norm_mod_kernel,
        grid=(nt,),
        in_specs=[pl.BlockSpec((tm, d), lambda i: (i, 0)),
                  pl.BlockSpec((1, d), lambda i: (0, 0)),
                  pl.BlockSpec((1, gq, d), lambda i: (i, 0, 0)),
                  pl.BlockSpec((1, gq, d), lambda i: (i, 0, 0))],
        out_specs=pl.BlockSpec((tm, d), lambda i: (i, 0)),
        out_shape=jax.ShapeDtypeStruct((r, d), BF16),
        compiler_params=_cparams("arbitrary"),
        name="norm_mod",
    )(x, g.reshape(1, d), sc, sh)


_ROW_OUTS = (("xb", RG_WIDTH, F32), ("gb", RG_WIDTH, F32), ("qc", C_QK_WIDTH, F32), ("kc", C_QK_WIDTH, F32),
             ("vc", C_V_WIDTH, BF16), ("gc", C_V_WIDTH, F32), ("gl", None, F32))


def _write_row_outs(x, wnn_ref, offs_nn, refs):
    for (name, _, dt), ref in zip(_ROW_OUTS, refs):
        a, b = offs_nn[name]
        ref[...] = _dot(x, wnn_ref[:, a:b]).astype(dt)


def _in_proj_prompt_kernel(offs_nn, offs_nt, x_ref, wnn_ref, wnt_ref, bfr_ref, bfc_ref, esel_ref,
                           qt_ref, ka_ref, vtb_ref, ktf_ref, vtf_ref, lft_ref, *rest):
    row_refs, (carry_c, carry_r) = rest[:len(_ROW_OUTS)], rest[len(_ROW_OUTS):]
    i = pl.program_id(1)
    x = x_ref[...]
    tm = x.shape[0]

    @pl.when(i == 0)
    def _():
        carry_c[...] = jnp.zeros(carry_c.shape, F32)
        carry_r[...] = jnp.zeros(carry_r.shape, F32)

    def nn(name):
        a, b = offs_nn[name]
        return _dot(x, wnn_ref[:, a:b])

    def nt(name):
        a, b = offs_nt[name]
        return _dot_nt(wnt_ref[a:b, :], x)

    lf_c = _log_sigmoid(nn("fa") + bfr_ref[...])
    cum_c = _cumsum(lf_c, 0) + carry_c[...]
    carry_c[...] = cum_c[tm - 1:tm, :]
    lf_r = _log_sigmoid(nt("fa") + bfc_ref[...])
    cum_r = _cumsum(lf_r, 1) + carry_r[...]
    carry_r[...] = cum_r[:, tm - 1:tm]
    lft_ref[0] = lf_r[:A_HEADS, :]

    cs = jnp.concatenate([p.astype(BF16) for p in _split3(cum_c)], axis=1)
    hi_r, mid_r, lo_r = _split3(cum_r)
    kp = nn("kpad")
    lane = lax.broadcasted_iota(jnp.int32, (tm, AUG_DIM), 1)
    ones_k = jnp.where(lane >= BIAS_LANE, jnp.where(lane < BIAS_LANE + N_SPLIT, 1.0, 0.0), 0.0)
    rowi = lax.broadcasted_iota(jnp.int32, (BF16_ROWS, tm), 0)
    pad_rows = jnp.zeros((AUG_DIM - A_HEAD_DIM - BF16_ROWS, tm), BF16)
    q_t = nt("qa")
    k_t = nt("ka")
    v_t = nt("va")
    for h in range(A_HEADS):
        hs = slice(h * A_HEAD_DIM, (h + 1) * A_HEAD_DIM)
        ka_ref[0, h] = (kp[:, h * LANES:(h + 1) * LANES] + _dot(cs, esel_ref[h]) + ones_k).astype(BF16)
        bias = jnp.where(rowi == 0, hi_r[h:h + 1, :],
                         jnp.where(rowi == 1, mid_r[h:h + 1, :],
                                   jnp.where(rowi == 2, lo_r[h:h + 1, :],
                                             jnp.where(rowi < 2 * N_SPLIT, 1.0, 0.0))))
        qt_ref[0, h] = jnp.concatenate([q_t[hs, :].astype(BF16), bias.astype(BF16), pad_rows], axis=0)
        vtb_ref[0, h] = v_t[hs, :].astype(BF16)
        ktf_ref[0, h] = k_t[hs, :]
        vtf_ref[0, h] = v_t[hs, :]
    _write_row_outs(x, wnn_ref, offs_nn, row_refs)


def _in_proj_sample_kernel(offs_nn, offs_nt, x_ref, wnn_ref, wnt_ref, bfr_ref,
                           q_ref, kf_ref, vf_ref, lf_ref, *row_refs):
    x = x_ref[...]

    def nt(name):
        a, b = offs_nt[name]
        return _dot_nt(x, wnt_ref[a:b, :])

    q_ref[...] = nt("qa").astype(BF16)
    kf_ref[...] = nt("ka")
    vf_ref[...] = nt("va")
    a, b = offs_nn["fa"]
    lf_ref[...] = _log_sigmoid(_dot(x, wnn_ref[:, a:b]) + bfr_ref[...])
    _write_row_outs(x, wnn_ref, offs_nn, row_refs)


def _resident(shape, nd):
    zeros = tuple(0 for _ in shape)
    if nd == 1:
        return pl.BlockSpec(shape, lambda i: zeros, pipeline_mode=pl.Buffered(1))
    return pl.BlockSpec(shape, lambda b, i: zeros, pipeline_mode=pl.Buffered(1))


def _row_out_specs(r, d, tm, index):
    specs, shapes = [], []
    for _, w, dt in _ROW_OUTS:
        w = N_BRANCH * d if w is None else w
        specs.append(pl.BlockSpec((tm, w), index))
        shapes.append(jax.ShapeDtypeStruct((r, w), dt))
    return specs, shapes


def _in_proj_prompt(hn, w_nn, w_nt, bf_row, bf_col, esel, offs_nn, offs_nt, n, s, tm):
    r, d = hn.shape
    nt_ = s // tm
    row_specs, row_shapes = _row_out_specs(r, d, tm, lambda b, i: (b * nt_ + i, 0))
    fm = lambda rows, dt: (pl.BlockSpec((1, A_HEADS, rows, tm), lambda b, i: (b, 0, 0, i)),
                           jax.ShapeDtypeStruct((n, A_HEADS, rows, s), dt))
    outs = [fm(AUG_DIM, BF16),
            (pl.BlockSpec((1, A_HEADS, tm, AUG_DIM), lambda b, i: (b, 0, i, 0)),
             jax.ShapeDtypeStruct((n, A_HEADS, s, AUG_DIM), BF16)),
            fm(A_HEAD_DIM, BF16), fm(A_HEAD_DIM, F32), fm(A_HEAD_DIM, F32),
            (pl.BlockSpec((1, A_HEADS, tm), lambda b, i: (b, 0, i)), jax.ShapeDtypeStruct((n, A_HEADS, s), F32))]
    return pl.pallas_call(
        functools.partial(_in_proj_prompt_kernel, offs_nn, offs_nt),
        grid=(n, nt_),
        in_specs=[pl.BlockSpec((tm, d), lambda b, i: (b * nt_ + i, 0)),
                  _resident(w_nn.shape, 2), _resident(w_nt.shape, 2), _resident(bf_row.shape, 2),
                  _resident(bf_col.shape, 2), _resident(esel.shape, 2)],
        out_specs=[o[0] for o in outs] + row_specs,
        out_shape=[o[1] for o in outs] + row_shapes,
        scratch_shapes=[pltpu.VMEM((1, LANES), F32), pltpu.VMEM((BF16_ROWS, 1), F32)],
        compiler_params=_cparams("arbitrary", "arbitrary"),
        name="in_proj_prompt",
    )(hn, w_nn, w_nt, bf_row, bf_col, esel)


def _in_proj_sample(hn, w_nn, w_nt, bf_row, offs_nn, offs_nt, tm):
    r, d = hn.shape
    row_specs, row_shapes = _row_out_specs(r, d, tm, lambda i: (i, 0))
    row = lambda w, dt: (pl.BlockSpec((tm, w), lambda i: (i, 0)), jax.ShapeDtypeStruct((r, w), dt))
    outs = [row(A_WIDTH, BF16), row(A_WIDTH, F32), row(A_WIDTH, F32), row(LANES, F32)]
    return pl.pallas_call(
        functools.partial(_in_proj_sample_kernel, offs_nn, offs_nt),
        grid=(r // tm,),
        in_specs=[pl.BlockSpec((tm, d), lambda i: (i, 0)),
                  _resident(w_nn.shape, 1), _resident(w_nt.shape, 1), _resident(bf_row.shape, 1)],
        out_specs=[o[0] for o in outs] + row_specs,
        out_shape=[o[1] for o in outs] + row_shapes,
        compiler_params=_cparams("arbitrary"),
        name="in_proj_sample",
    )(hn, w_nn, w_nt, bf_row)


def _flash_kernel(t, q_ref, k_ref, v_ref, o_ref):
    qi = pl.program_id(1)
    key = lax.broadcasted_iota(jnp.int32, (t, t), 0)
    qry = lax.broadcasted_iota(jnp.int32, (t, t), 1)
    heads = range(A_HEADS)

    def step(kb, carries, diagonal):
        start = pl.multiple_of(kb * t, t)
        scores = [_dot(k_ref[0, j, pl.ds(start, t), :], q_ref[0, j]) for j in heads]
        probs, stats = [], []
        for j in heads:
            m, l, acc = carries[j]
            s = jnp.where(key <= qry, scores[j], -jnp.inf) if diagonal else scores[j]
            m_new = jnp.maximum(m, jnp.max(s, axis=0, keepdims=True))
            p = jnp.exp(s - m_new)
            alpha = jnp.exp(m - m_new)
            probs.append(p.astype(BF16))
            stats.append((m_new, alpha * l + jnp.sum(p, axis=0, keepdims=True), alpha, acc))
        out = []
        for j in heads:
            m_new, l, alpha, acc = stats[j]
            out.append((m_new, l, alpha * acc + _dot(v_ref[0, j, :, pl.ds(start, t)], probs[j])))
        return tuple(out)

    init = tuple((jnp.full((1, t), -jnp.inf, F32), jnp.zeros((1, t), F32), jnp.zeros((A_HEAD_DIM, t), F32))
                 for _ in heads)
    carries = lax.fori_loop(0, qi, lambda kb, c: step(kb, c, False), init)
    final = step(qi, carries, True)
    o_t = jnp.concatenate([acc / l for _, l, acc in final], axis=0)
    o_ref[...] = o_t.T.astype(BF16)


def _fox_prompt(qt_aug, k_aug, vt, n, s):
    t = min(FLASH_TILE, s)
    nq = s // t
    once = dict(pipeline_mode=pl.Buffered(1))
    return pl.pallas_call(
        functools.partial(_flash_kernel, t),
        grid=(n, nq),
        in_specs=[pl.BlockSpec((1, A_HEADS, AUG_DIM, t), lambda b, i: (b, 0, 0, i)),
                  pl.BlockSpec((1, A_HEADS, s, AUG_DIM), lambda b, i: (b, 0, 0, 0), **once),
                  pl.BlockSpec((1, A_HEADS, A_HEAD_DIM, s), lambda b, i: (b, 0, 0, 0), **once)],
        out_specs=pl.BlockSpec((t, A_WIDTH), lambda b, i: (b * nq + i, 0)),
        out_shape=jax.ShapeDtypeStruct((n * s, A_WIDTH), BF16),
        compiler_params=_cparams("arbitrary", "arbitrary"),
        name="fox_prompt",
    )(qt_aug, k_aug, vt)


def _paged_kernel(pps, page, pt_ref, q_ref, kn_ref, vn_ref, lfn_ref, *rest):
    k_refs = rest[:pps]
    v_refs = rest[pps:2 * pps]
    lf_refs = rest[2 * pps:3 * pps]
    o_ref = rest[3 * pps]
    qbd_ref, cq_ref, negc_ref, m_ref, l_ref, acc_ref, run_ref = rest[3 * pps + 1:]
    j = pl.program_id(1)
    nj = pl.num_programs(1)
    l_new = q_ref.shape[1]
    rows = A_HEADS * l_new
    lane_w = lax.broadcasted_iota(jnp.int32, (l_new, A_WIDTH), 1)
    row_i = lax.broadcasted_iota(jnp.int32, (rows, page), 0)
    lane_i = lax.broadcasted_iota(jnp.int32, (rows, page), 1)
    tok_of_row = row_i % l_new

    def expand(x):
        return jnp.broadcast_to(x[:, None, :], (A_HEADS, l_new, page)).reshape(rows, page)

    @pl.when(j == 0)
    def _():
        q = q_ref[0].astype(F32)
        qbd_ref[...] = jnp.concatenate(
            [jnp.where((lane_w // A_HEAD_DIM) == h, q, 0.0) for h in range(A_HEADS)], axis=0).astype(BF16)
        c = lfn_ref[0]
        lane = lax.broadcasted_iota(jnp.int32, c.shape, 1)
        s = 1
        while s < l_new:
            c = c + jnp.where(lane >= s, pltpu.roll(c, s, axis=1), 0.0)
            s *= 2
        ce = expand(c)
        negc_ref[...] = -ce
        cq_ref[...] = jnp.sum(jnp.where(lane_i == tok_of_row, ce, 0.0), axis=-1, keepdims=True)
        m_ref[...] = jnp.full(m_ref.shape, -jnp.inf, F32)
        l_ref[...] = jnp.zeros(l_ref.shape, F32)
        acc_ref[...] = jnp.zeros(acc_ref.shape, F32)
        run_ref[...] = jnp.zeros(run_ref.shape, F32)

    qbd = qbd_ref[...]
    cq = cq_ref[...]

    def update(s, pv_fn):
        m = m_ref[...]
        m_new = jnp.maximum(m, jnp.max(s, axis=-1, keepdims=True))
        p = jnp.exp(s - m_new)
        alpha = jnp.exp(m - m_new)
        l_ref[...] = alpha * l_ref[...] + jnp.sum(p, axis=-1, keepdims=True)
        acc_ref[...] = alpha * acc_ref[...] + pv_fn(p.astype(BF16))
        m_ref[...] = m_new

    incl, lfs = [], []
    for i in range(pps):
        lf = lf_refs[i][0, 0]
        lfs.append(lf)
        incl.append(_cumsum(lf, 1, reverse=True))
    run = run_ref[...]
    later = [None] * pps
    for i in reversed(range(pps)):
        later[i] = run
        run = run + incl[i][:, 0:1]
    run_ref[...] = run
    scores = [_dot(qbd, k_refs[i][0, 0].astype(BF16)) + expand(incl[i] - lfs[i] + later[i]) for i in range(pps)]

    def pv_pages(p):
        out = None
        for i in range(pps):
            c = _dot_nt(p[:, i * page:(i + 1) * page], v_refs[i][0, 0].astype(BF16))
            out = c if out is None else out + c
        return out

    update(jnp.concatenate(scores, axis=1) + cq, pv_pages)

    @pl.when(j == nj - 1)
    def _():
        pad = jnp.zeros((page - l_new, A_WIDTH), BF16)
        kn = jnp.concatenate([kn_ref[0].astype(BF16), pad], axis=0)
        vn = jnp.concatenate([vn_ref[0].astype(BF16), pad], axis=0)
        s = _dot_nt(qbd, kn) + negc_ref[...] + cq
        s = jnp.where(lane_i <= tok_of_row, s, -jnp.inf)
        update(s, lambda p: _dot(p, vn))
        o = acc_ref[...] / l_ref[...]
        out = jnp.zeros((l_new, A_WIDTH), F32)
        for h in range(A_HEADS):
            sel = (lane_w // A_HEAD_DIM) == h
            out = out + jnp.where(sel, o[h * l_new:(h + 1) * l_new, :], 0.0)
        o_ref[0] = out.astype(BF16)


def _fox_sample(layer, q, k_new, v_new, lfn_t, cache_kt, cache_vt, cache_lft, page_table):
    b, l_new, _ = q.shape
    page = cache_kt.shape[3]
    n_pages = page_table.shape[1]
    pps = math.gcd(PAGES_PER_STEP, n_pages)
    nj = n_pages // pps
    rows = A_HEADS * l_new

    def page_map(i):
        return lambda bi, j, pt: (layer, pt[bi * n_pages + (nj - 1 - j) * pps + i], 0, 0)

    seq = lambda w: pl.BlockSpec((1, l_new, w), lambda bi, j, pt: (bi, 0, 0))
    in_specs = [seq(A_WIDTH), seq(A_WIDTH), seq(A_WIDTH),
                pl.BlockSpec((1, A_HEADS, page), lambda bi, j, pt: (bi, 0, 0))]
    in_specs += [pl.BlockSpec((1, 1, A_WIDTH, page), page_map(i)) for i in range(pps)]
    in_specs += [pl.BlockSpec((1, 1, A_WIDTH, page), page_map(i)) for i in range(pps)]
    in_specs += [pl.BlockSpec((1, 1, A_HEADS, page), page_map(i)) for i in range(pps)]
    grid_spec = pltpu.PrefetchScalarGridSpec(
        num_scalar_prefetch=1,
        grid=(b, nj),
        in_specs=in_specs,
        out_specs=pl.BlockSpec((1, l_new, A_WIDTH), lambda bi, j, pt: (bi, 0, 0)),
        scratch_shapes=[pltpu.VMEM((rows, A_WIDTH), BF16), pltpu.VMEM((rows, 1), F32),
                        pltpu.VMEM((rows, page), F32), pltpu.VMEM((rows, 1), F32),
                        pltpu.VMEM((rows, 1), F32), pltpu.VMEM((rows, A_WIDTH), F32),
                        pltpu.VMEM((A_HEADS, 1), F32)],
    )
    return pl.pallas_call(
        functools.partial(_paged_kernel, pps, page),
        grid_spec=grid_spec,
        out_shape=jax.ShapeDtypeStruct((b, l_new, A_WIDTH), BF16),
        compiler_params=_cparams("arbitrary", "arbitrary"),
        name="fox_sample",
    )(page_table.reshape(-1), q, k_new, v_new, lfn_t,
      *([cache_kt] * pps), *([cache_vt] * pps), *([cache_lft] * pps))


def _rglru_kernel(tl, xb_ref, gb_ref, cs_ref, h0_ref, cw_ref, cb_ref, wa_ref, ba_ref, wi_ref, bi_ref, lam_ref,
                  y_ref, cn_ref, hn_ref, xpad, hcar):
    i = pl.program_id(1)
    hist = CONV_WIDTH - 1
    base = SUBLANES

    @pl.when(i == 0)
    def _():
        xpad[base - hist:base, :] = cs_ref[0]
        hcar[...] = h0_ref[0]

    x = xb_ref[...]
    xpad[base:base + tl, :] = x
    xc = cb_ref[...] + sum(xpad[base - hist + j:base - hist + j + tl, :] * cw_ref[j:j + 1, :]
                           for j in range(CONV_WIDTH))
    tail = xpad[base + tl - hist:base + tl, :]
    xpad[base - hist:base, :] = tail
    cn_ref[0] = tail

    xcb = xc.astype(BF16)
    r = jax.nn.sigmoid(_dot(xcb, wa_ref[...]) + ba_ref[...])
    ig = jax.nn.sigmoid(_dot(xcb, wi_ref[...]) + bi_ref[...])
    log_a = -LRU_C * r * _softplus(-lam_ref[...])
    a = jnp.exp(log_a)
    x2 = 2.0 * log_a
    e2 = jnp.exp(x2)
    em1 = jnp.where(e2 == 1.0, x2, jnp.where(x2 < -1.0, e2 - 1.0, (e2 - 1.0) * x2 / jnp.log(e2)))
    u = jnp.sqrt(-em1) * (ig * xc)

    row = lax.broadcasted_iota(jnp.int32, (tl, RG_WIDTH), 0)
    aa, bb = a, u
    s = 1
    while s < tl:
        keep = row >= s
        a_sh = jnp.where(keep, pltpu.roll(aa, s, axis=0), 1.0)
        b_sh = jnp.where(keep, pltpu.roll(bb, s, axis=0), 0.0)
        bb = aa * b_sh + bb
        aa = aa * a_sh
        s *= 2
    h = aa * hcar[...] + bb
    hlast = h[tl - 1:tl, :]
    hcar[...] = hlast
    hn_ref[0] = hlast
    y_ref[...] = (h * jax.nn.gelu(gb_ref[...])).astype(BF16)


def _rglru(xb, gb, conv_state, h0, cw, cb, wa_bd, ba, wi_bd, bi, lam, n, l):
    tl = min(SCAN_TILE, l)
    nt = l // tl
    w = RG_WIDTH
    hist = CONV_WIDTH - 1
    full = lambda shp: pl.BlockSpec(shp, lambda b, i: tuple(0 for _ in shp))
    return pl.pallas_call(
        functools.partial(_rglru_kernel, tl),
        grid=(n, nt),
        in_specs=[pl.BlockSpec((tl, w), lambda b, i: (b * nt + i, 0)),
                  pl.BlockSpec((tl, w), lambda b, i: (b * nt + i, 0)),
                  pl.BlockSpec((1, hist, w), lambda b, i: (b, 0, 0)),
                  pl.BlockSpec((1, 1, w), lambda b, i: (b, 0, 0)),
                  full((CONV_WIDTH, w)), full((1, w)), full((w, w)), full((1, w)), full((w, w)), full((1, w)),
                  full((1, w))],
        out_specs=[pl.BlockSpec((tl, w), lambda b, i: (b * nt + i, 0)),
                   pl.BlockSpec((1, hist, w), lambda b, i: (b, 0, 0)),
                   pl.BlockSpec((1, 1, w), lambda b, i: (b, 0, 0))],
        out_shape=[jax.ShapeDtypeStruct((n * l, w), BF16),
                   jax.ShapeDtypeStruct((n, hist, w), F32),
                   jax.ShapeDtypeStruct((n, 1, w), F32)],
        scratch_shapes=[pltpu.VMEM((tl + SUBLANES, w), F32), pltpu.VMEM((1, w), F32)],
        compiler_params=_cparams("arbitrary", "arbitrary"),
        name="rglru",
    )(xb, gb, conv_state, h0.reshape(n, 1, w), cw, cb.reshape(1, w), wa_bd, ba.reshape(1, w), wi_bd,
      bi.reshape(1, w), lam.reshape(1, w))


def _retention_kernel(t, qc_ref, kc_ref, vc_ref, gc_ref, cos_ref, sin_ref, qdec_ref, kdec_ref, dmask_ref,
                      cdec_ref, s0_ref, y_ref, sn_ref, st):
    c = pl.program_id(1)
    nc = pl.num_programs(1)

    @pl.when(c == 0)
    def _():
        st[...] = s0_ref[0]

    lane = lax.broadcasted_iota(jnp.int32, (t, C_QK_WIDTH), 1)
    first_half = (lane % C_QK_DIM) < (C_QK_DIM // 2)
    cos = cos_ref[...]
    sin = sin_ref[...]
    half = C_QK_DIM // 2

    def rot(x):
        swapped = jnp.where(first_half, pltpu.roll(x, C_QK_WIDTH - half, axis=1), pltpu.roll(x, half, axis=1))
        return x * cos + swapped * sin

    qr = rot(qc_ref[...])
    kr = rot(kc_ref[...]) * (C_QK_DIM ** -0.5)
    qb = qr.astype(BF16)
    kb = kr.astype(BF16)
    qd = (qr * qdec_ref[...]).astype(BF16)
    kd = (kr * kdec_ref[...]).astype(BF16)
    v = vc_ref[...]
    outs = []
    for h in range(C_HEADS):
        qs = slice(h * C_QK_DIM, (h + 1) * C_QK_DIM)
        vh = v[:, h * C_V_DIM:(h + 1) * C_V_DIM]
        sc = _dot_nt(qb[:, qs], kb[:, qs]) * dmask_ref[h]
        sh = st[h]
        o = _dot(sc.astype(BF16), vh) + _dot(qd[:, qs], sh.astype(BF16))
        st[h] = cdec_ref[h] * sh + _dot_tn(kd[:, qs], vh)
        outs.append(_rms(o))
    g = gc_ref[...]
    y_ref[...] = ((g * jax.nn.sigmoid(g)) * jnp.concatenate(outs, axis=1)).astype(BF16)

    @pl.when(c == nc - 1)
    def _():
        sn_ref[0] = st[...]


def _retention_tables(l, pos0):
    t = RET_CHUNK if l % RET_CHUNK == 0 else l
    half = C_QK_DIM // 2
    inv = ROPE_BASE ** (-jnp.arange(half, dtype=F32) / half)
    ang = (pos0 + jnp.arange(l)).astype(F32)[:, None] * inv[None, :]
    cos = jnp.tile(jnp.cos(ang), (1, 2 * C_HEADS))
    sin = jnp.tile(jnp.concatenate([-jnp.sin(ang), jnp.sin(ang)], axis=1), (1, C_HEADS))
    log_g = jnp.log1p(-jnp.exp2(-5.0 - jnp.arange(C_HEADS, dtype=F32)))
    idx = jnp.arange(t, dtype=F32)
    diff = idx[:, None] - idx[None, :]
    dmask = jnp.where(diff >= 0, jnp.exp(log_g[:, None, None] * jnp.maximum(diff, 0.0)), 0.0)
    kdec = jnp.repeat(jnp.exp(log_g[None, :] * (t - 1.0 - idx)[:, None]), C_QK_DIM, axis=1)
    qdec = jnp.repeat(jnp.exp(log_g[None, :] * (idx[:, None] + 1.0)), C_QK_DIM, axis=1)
    cdec = jnp.broadcast_to(jnp.exp(log_g * t)[:, None, None], (C_HEADS, 1, C_V_DIM))
    return t, cos, sin, qdec, kdec, dmask, cdec


def _retention(qc, kc, vc, gc, s0, n, l, pos0):
    t, cos, sin, qdec, kdec, dmask, cdec = _retention_tables(l, pos0)
    nc = l // t
    full = lambda shp: pl.BlockSpec(shp, lambda b, c: tuple(0 for _ in shp))
    rowspec = lambda w: pl.BlockSpec((t, w), lambda b, c: (b * nc + c, 0))
    return pl.pallas_call(
        functools.partial(_retention_kernel, t),
        grid=(n, nc),
        in_specs=[rowspec(C_QK_WIDTH), rowspec(C_QK_WIDTH), rowspec(C_V_WIDTH), rowspec(C_V_WIDTH),
                  pl.BlockSpec((t, C_QK_WIDTH), lambda b, c: (c, 0)),
                  pl.BlockSpec((t, C_QK_WIDTH), lambda b, c: (c, 0)),
                  full((t, C_QK_WIDTH)), full((t, C_QK_WIDTH)), full((C_HEADS, t, t)),
                  full((C_HEADS, 1, C_V_DIM)),
                  pl.BlockSpec((1, C_HEADS, C_QK_DIM, C_V_DIM), lambda b, c: (b, 0, 0, 0))],
        out_specs=[rowspec(C_V_WIDTH),
                   pl.BlockSpec((1, C_HEADS, C_QK_DIM, C_V_DIM), lambda b, c: (b, 0, 0, 0))],
        out_shape=[jax.ShapeDtypeStruct((n * l, C_V_WIDTH), BF16),
                   jax.ShapeDtypeStruct((n, C_HEADS, C_QK_DIM, C_V_DIM), F32)],
        scratch_shapes=[pltpu.VMEM((C_HEADS, C_QK_DIM, C_V_DIM), F32)],
        compiler_params=_cparams("arbitrary", "arbitrary"),
        name="retention",
    )(qc, kc, vc, gc, cos, sin, qdec, kdec, dmask, cdec, s0)


def _merge_kernel(n_exp, oa_ref, ob_ref, oc_ref, gl_ref, x_ref, g1_ref, sc_ref, sh_ref, nf_ref, wbr_ref, wo_ref,
                  wr_ref, br_ref, xm_ref, hp_ref, ti_ref, tw_ref):
    rows, d = x_ref.shape
    m = jnp.zeros((rows, d), F32)
    for b, (o_ref, w0) in enumerate(((oa_ref, 0), (ob_ref, A_WIDTH), (oc_ref, A_WIDTH + RG_WIDTH))):
        width = o_ref.shape[1]
        gate = jax.nn.sigmoid(gl_ref[:, b * d:(b + 1) * d])
        m = m + gate * _dot(o_ref[...], wbr_ref[w0:w0 + width, :])
    x = x_ref[...] + _expand_groups(g1_ref[0], rows) * _dot(m.astype(BF16), wo_ref[...])
    xm_ref[...] = x
    hn = _rms(x) * nf_ref[...]
    hn = (hn * (1.0 + _expand_groups(sc_ref[0], rows)) + _expand_groups(sh_ref[0], rows)).astype(BF16)
    bits = pltpu.bitcast(hn.astype(F32), U32)
    hp_ref[...] = bits[:, :d // 2] | (bits[:, d // 2:] >> 16)
    lane = lax.broadcasted_iota(jnp.int32, (rows, LANES), 1)
    logits = jnp.where(lane < n_exp, _dot(hn, wr_ref[...]) + br_ref[...], -jnp.inf)
    vals, idxs = [], []
    for _ in range(TOP_K):
        mx = jnp.max(logits, axis=-1, keepdims=True)
        ix = jnp.min(jnp.where(logits == mx, lane, LANES), axis=-1, keepdims=True)
        vals.append(mx)
        idxs.append(ix)
        logits = jnp.where(lane == ix, -jnp.inf, logits)
    es = [jnp.exp(v - vals[0]) for v in vals]
    den = es[0]
    for e in es[1:]:
        den = den + e
    ti = jnp.zeros((rows, LANES), jnp.int32)
    tw = jnp.zeros((rows, LANES), F32)
    for k in range(TOP_K):
        ti = jnp.where(lane == k, idxs[k], ti)
        tw = jnp.where(lane == k, es[k] / den, tw)
    ti_ref[...] = ti
    tw_ref[...] = tw


def _merge(oa, ob, oc, gl, x, g1, sc2, sh2, nf, wbr, wo, wr_pad, br_pad, n_exp, tm):
    r, d = x.shape
    nt, gq, _ = g1.shape
    rowspec = lambda w: pl.BlockSpec((tm, w), lambda i: (i, 0))
    full = lambda shp: pl.BlockSpec(shp, lambda i: tuple(0 for _ in shp))
    mod = pl.BlockSpec((1, gq, d), lambda i: (i, 0, 0))
    return pl.pallas_call(
        functools.partial(_merge_kernel, n_exp),
        grid=(nt,),
        in_specs=[rowspec(A_WIDTH), rowspec(RG_WIDTH), rowspec(C_V_WIDTH), rowspec(N_BRANCH * d), rowspec(d),
                  mod, mod, mod, full((1, d)), full(wbr.shape), full((d, d)), full((d, LANES)),
                  full((1, LANES))],
        out_specs=[rowspec(d), rowspec(d // 2), rowspec(LANES), rowspec(LANES)],
        out_shape=[jax.ShapeDtypeStruct((r, d), F32), jax.ShapeDtypeStruct((r, d // 2), U32),
                   jax.ShapeDtypeStruct((r, LANES), jnp.int32), jax.ShapeDtypeStruct((r, LANES), F32)],
        compiler_params=_cparams("arbitrary"),
        name="merge",
    )(oa, ob, oc, gl, x, g1, sc2, sh2, nf.reshape(1, d), wbr, wo, wr_pad, br_pad)


def _expert_kernel(te_ref, x_ref, w_ref, wg_ref, bg_ref, wu_ref, bu_ref, wd_ref, bd_ref, y_ref):
    words = x_ref[...]
    x = jnp.concatenate([pltpu.bitcast(words & jnp.uint32(0xFFFF0000), F32).astype(BF16),
                         pltpu.bitcast(words << 16, F32).astype(BF16)], axis=1)
    w = w_ref[...]
    g = _dot(x, wg_ref[0]) + bg_ref[0]
    u = _dot(x, wu_ref[0]) + bu_ref[0]
    g = jnp.minimum(g, SWIGLU_LIMIT)
    u = jnp.clip(u, -SWIGLU_LIMIT, SWIGLU_LIMIT)
    hid = (u + 1.0) * g * jax.nn.sigmoid(SWIGLU_ALPHA * g) * w
    y_ref[...] = _dot(hid.astype(BF16), wd_ref[0]) + w * bd_ref[0]


def _experts(layer, tile_expert, xs, ws, wg, bg, wu, bu, wd, bd, tm):
    p, half_d = xs.shape
    depth, n_exp, d, f = wg.shape
    pick = lambda i, te: (layer * n_exp + te[i], 0, 0)
    flat = lambda a: a.reshape((depth * n_exp,) + a.shape[2:])
    bias = lambda a: a.reshape(depth * n_exp, 1, a.shape[-1])
    grid_spec = pltpu.PrefetchScalarGridSpec(
        num_scalar_prefetch=1,
        grid=(p // tm,),
        in_specs=[pl.BlockSpec((tm, half_d), lambda i, te: (i, 0)),
                  pl.BlockSpec((tm, 1), lambda i, te: (i, 0)),
                  pl.BlockSpec((1, d, f), pick), pl.BlockSpec((1, 1, f), pick),
                  pl.BlockSpec((1, d, f), pick), pl.BlockSpec((1, 1, f), pick),
                  pl.BlockSpec((1, f, d), pick), pl.BlockSpec((1, 1, d), pick)],
        out_specs=pl.BlockSpec((tm, d), lambda i, te: (i, 0)),
    )
    return pl.pallas_call(
        _expert_kernel,
        grid_spec=grid_spec,
        out_shape=jax.ShapeDtypeStruct((p, d), F32),
        compiler_params=_cparams("arbitrary"),
        name="experts",
    )(tile_expert, xs, ws, flat(wg), bias(bg), flat(wu), bias(bu), flat(wd), bias(bd))


def _dispatch_plan(top_i, top_w, n_exp, tm):
    t_all = top_i.shape[0]
    a = t_all * TOP_K
    assert a % LANES == 0
    e = top_i.reshape(a)
    w = top_w.reshape(a)
    n_tiles = -(-a // tm) + n_exp
    onehot = (e.reshape(-1, LANES, 1) == jnp.arange(n_exp, dtype=jnp.int32)).astype(BF16)
    tri = (jnp.arange(LANES)[:, None] >= jnp.arange(LANES)[None, :]).astype(BF16)
    within = jnp.einsum("ij,bjk->bik", tri, onehot, preferred_element_type=F32)
    blk_tot = within[:, -1, :]
    nb = blk_tot.shape[0]
    tri_b = (jnp.arange(nb)[:, None] >= jnp.arange(nb)[None, :]).astype(BF16)
    blk_end = jnp.dot(tri_b, blk_tot.astype(BF16), preferred_element_type=F32)
    counts = blk_end[-1].astype(jnp.int32)
    padded = ((counts + tm - 1) // tm) * tm
    g_end = jnp.cumsum(padded)
    g_start = g_end - padded
    c_start = jnp.cumsum(counts) - counts
    base = (blk_end - blk_tot)[:, None, :] + g_start.astype(F32)[None, None, :]
    pos = (jnp.sum(onehot.astype(F32) * (within + base), axis=-1) - 1.0).astype(jnp.int32).reshape(t_all, TOP_K)
    order = jnp.argsort(e, stable=True).astype(jnp.int32)
    tile_start = jnp.arange(n_tiles, dtype=jnp.int32) * tm
    tile_expert = jnp.sum((tile_start[:, None] >= g_end[None, :]).astype(jnp.int32), axis=1)
    tile_expert = jnp.minimum(tile_expert, n_exp - 1)
    off = (tile_start - g_start[tile_expert])[:, None] + jnp.arange(tm, dtype=jnp.int32)[None, :]
    valid = (off < counts[tile_expert][:, None]) & (tile_start < g_end[n_exp - 1])[:, None]
    src = jnp.take(order, jnp.clip(c_start[tile_expert][:, None] + off, 0, a - 1).reshape(-1), mode="clip")
    valid = valid.reshape(-1)
    slot_token = jnp.where(valid, src // TOP_K, 0)
    slot_w = jnp.where(valid, jnp.take(w, src, mode="clip"), 0.0)
    return tile_expert, slot_token, slot_w.reshape(-1, 1), pos


def _combine_kernel(final, yg_ref, x_ref, g2_ref, gn_ref, sc_ref, sh_ref, *outs):
    rows = x_ref.shape[0]
    moe = yg_ref[0]
    for k in range(1, TOP_K):
        moe = moe + yg_ref[k]
    x = x_ref[...] + _expand_groups(g2_ref[0], rows) * moe
    y = _rms(x) * gn_ref[...]
    if final:
        outs[0][...] = y
    else:
        outs[0][...] = x
        y = y * (1.0 + _expand_groups(sc_ref[0], rows)) + _expand_groups(sh_ref[0], rows)
        outs[1][...] = y.astype(BF16)


def _combine(yg, tile0, x, g2, gn, sc, sh, final, tm):
    r, d = x.shape
    nt, gq, _ = g2.shape
    rowspec = pl.BlockSpec((tm, d), lambda i: (i, 0))
    mod = pl.BlockSpec((1, gq, d), lambda i: (i, 0, 0))
    if final:
        out_specs, out_shape = [rowspec], [jax.ShapeDtypeStruct((r, d), F32)]
    else:
        out_specs = [rowspec, rowspec]
        out_shape = [jax.ShapeDtypeStruct((r, d), F32), jax.ShapeDtypeStruct((r, d), BF16)]
    return pl.pallas_call(
        functools.partial(_combine_kernel, final),
        grid=(nt,),
        in_specs=[pl.BlockSpec((TOP_K, tm, d), lambda i: (0, i + tile0, 0)), rowspec, mod,
                  pl.BlockSpec((1, d), lambda i: (0, 0)), mod, mod],
        out_specs=out_specs,
        out_shape=out_shape,
        compiler_params=_cparams("arbitrary"),
        name="combine",
    )(yg, x, g2, gn.reshape(1, d), sc, sh)


def _prep_in_proj(w_in_l, b_f_l, d):
    sizes = (A_WIDTH, A_WIDTH, A_WIDTH, A_HEADS, RG_WIDTH, RG_WIDTH, C_QK_WIDTH, C_QK_WIDTH, C_V_WIDTH,
             C_V_WIDTH, N_BRANCH * d)
    names = ("qa", "ka", "va", "fa", "xb", "gb", "qc", "kc", "vc", "gc", "gl")
    src, o = {}, 0
    for nm, sz in zip(names, sizes):
        src[nm] = w_in_l[:, o:o + sz]
        o += sz
    kpad = jnp.pad(src["ka"].reshape(d, A_HEADS, A_HEAD_DIM),
                   ((0, 0), (0, 0), (0, LANES - A_HEAD_DIM))).reshape(d, A_HEADS * LANES)
    fa = jnp.pad(src["fa"], ((0, 0), (0, LANES - A_HEADS)))
    nn_parts = [("kpad", kpad)] + [(nm, src[nm]) for nm, _, _ in _ROW_OUTS] + [("fa", fa)]
    nt_parts = [("qa", (src["qa"] * (A_HEAD_DIM ** -0.5)).T), ("ka", src["ka"].T), ("va", src["va"].T),
                ("fa", jnp.pad(src["fa"].T, ((0, BF16_ROWS - A_HEADS), (0, 0))))]

    def pack(parts, axis):
        offs, o = {}, 0
        for nm, p_ in parts:
            offs[nm] = (o, o + p_.shape[axis])
            o += p_.shape[axis]
        return jnp.concatenate([p_ for _, p_ in parts], axis=axis).astype(BF16), offs

    w_nn, offs_nn = pack(nn_parts, 1)
    w_nt, offs_nt = pack(nt_parts, 0)
    bf_row = jnp.pad(b_f_l, (0, LANES - A_HEADS)).reshape(1, LANES)
    bf_col = jnp.pad(b_f_l, (0, BF16_ROWS - A_HEADS)).reshape(BF16_ROWS, 1)
    return w_nn, w_nt, offs_nn, offs_nt, bf_row, bf_col


def _bias_select():
    e = np.zeros((A_HEADS, N_SPLIT * LANES, AUG_DIM), np.float32)
    for h in range(A_HEADS):
        for p_ in range(N_SPLIT):
            e[h, p_ * LANES + h, BIAS_LANE + N_SPLIT + p_] = -1.0
    return jnp.asarray(e, BF16)


def _block_diag(w):
    nb, bi, bj = w.shape
    eye = jnp.eye(nb, dtype=w.dtype)
    return (eye[:, None, :, None] * w[:, :, None, :]).reshape(nb * bi, nb * bj)


def _tile_mods(m, rows_per_seq, tm):
    n_seq, d = m.shape
    if rows_per_seq >= tm:
        return jnp.repeat(m, rows_per_seq // tm, axis=0).reshape(-1, 1, d)
    return m.reshape(-1, tm // rows_per_seq, d)


def kernel(x_prompt, x_sample, c_prompt, c_sample, cache_k, cache_v, cache_logf, state_conv, state_h, state_ret, page_table, w_ada, b_ada, norm_mix, norm_ffn, w_in, b_f, conv_w, conv_b, w_rg_a, b_rg_a, w_rg_i, b_rg_i, lru_lambda, w_br, w_o, router_w, router_b, w_gate_e, b_gate_e, w_up_e, b_up_e, w_down_e, b_down_e, final_norm):
    n_p, seq, d = x_prompt.shape
    n_s, dec_seq, _ = x_sample.shape
    depth = w_ada.shape[0]
    n_pool, page = cache_k.shape[1], cache_k.shape[2]
    past_len = page_table.shape[1] * page
    n_exp = router_w.shape[-1]
    rp, rs = n_p * seq, n_s * dec_seq
    tm_p = min(ROW_TILE, rp)
    tm_s = min(ROW_TILE, rs)
    assert seq % tm_p == 0 and tm_p % SUBLANES == 0 and rs % tm_s == 0 and tm_s % dec_seq == 0
    assert dec_seq == SUBLANES and page == LANES and rp % ROW_TILE == 0

    n_c = n_p + n_s
    c_rows = -(-n_c // SUBLANES) * SUBLANES
    c_all = jnp.concatenate([c_prompt, c_sample, jnp.zeros((c_rows - n_c, d), F32)], axis=0)
    mod = _adaln(c_all, w_ada.astype(BF16), b_ada)

    def mods(layer, path):
        m = mod[layer, :n_p] if path == 0 else mod[layer, n_p:n_c]
        parts = jnp.split(m, 6, axis=-1)
        if path == 0:
            return [_tile_mods(p_, seq, tm_p) for p_ in parts]
        return [_tile_mods(p_, dec_seq, tm_s) for p_ in parts]

    xs_path = [x_prompt.reshape(rp, d), x_sample.reshape(rs, d)]
    geo = [(n_p, seq, tm_p, 0), (n_s, dec_seq, tm_s, past_len)]
    zero_state = [jnp.zeros((n_p, CONV_WIDTH - 1, RG_WIDTH), F32), jnp.zeros((n_p, RG_WIDTH), F32),
                  jnp.zeros((n_p, C_HEADS, C_QK_DIM, C_V_DIM), F32)]
    wr_pad = jnp.pad(router_w, ((0, 0), (0, 0), (0, LANES - n_exp))).astype(BF16)
    br_pad = jnp.pad(router_b, ((0, 0), (0, LANES - n_exp)))
    esel = _bias_select()
    wg_b, wu_b, wd_b = w_gate_e.astype(BF16), w_up_e.astype(BF16), w_down_e.astype(BF16)
    cache_kt = jnp.transpose(cache_k, (0, 1, 3, 4, 2)).reshape(depth, n_pool, A_WIDTH, page)
    cache_vt = jnp.transpose(cache_v, (0, 1, 3, 4, 2)).reshape(depth, n_pool, A_WIDTH, page)
    cache_lft = jnp.transpose(cache_logf, (0, 1, 3, 2))

    hn_path = [None, None]
    states = [[], []]
    for layer in range(depth):
        w_nn, w_nt, offs_nn, offs_nt, bf_row, bf_col = _prep_in_proj(w_in[layer], b_f[layer], d)
        wa_bd = _block_diag(w_rg_a[layer]).astype(BF16)
        wi_bd = _block_diag(w_rg_i[layer]).astype(BF16)
        wbr = w_br[layer].astype(BF16)
        wo = w_o[layer].astype(BF16)
        mid = [None, None]
        for path in (0, 1):
            n, l, tm, pos0 = geo[path]
            sh1, sc1, g1, sh2, sc2, g2 = mods(layer, path)
            if layer == 0:
                hn_path[path] = _norm_mod(xs_path[path], norm_mix[0], sc1, sh1, tm)
            if path == 0:
                outs = _in_proj_prompt(hn_path[0], w_nn, w_nt, bf_row, bf_col, esel, offs_nn, offs_nt, n, l, tm)
                qt_aug, k_aug, vt_b, kt_f, vt_f, lf_t = outs[:6]
                xb, gb, qc, kc, vc, gc, gl = outs[6:]
                oa = _fox_prompt(qt_aug, k_aug, vt_b, n, l)
                conv0, h0, s0 = zero_state
                st_k, st_v, st_f = kt_f, vt_f, lf_t
            else:
                outs = _in_proj_sample(hn_path[1], w_nn, w_nt, bf_row, offs_nn, offs_nt, tm)
                qrow, kf, vf, lf = outs[:4]
                xb, gb, qc, kc, vc, gc, gl = outs[4:]
                logf = lf[:, :A_HEADS].reshape(n, l, A_HEADS)
                lfn_t = jnp.pad(jnp.transpose(logf, (0, 2, 1)), ((0, 0), (0, 0), (0, page - l)))
                oa = _fox_sample(layer, qrow.reshape(n, l, A_WIDTH), kf.reshape(n, l, A_WIDTH),
                                 vf.reshape(n, l, A_WIDTH), lfn_t, cache_kt, cache_vt, cache_lft,
                                 page_table).reshape(n * l, A_WIDTH)
                conv0, h0, s0 = state_conv[layer], state_h[layer], state_ret[layer]
                st_k = kf.reshape(n, l, A_HEADS, A_HEAD_DIM)
                st_v = vf.reshape(n, l, A_HEADS, A_HEAD_DIM)
                st_f = logf
            ob, conv_new, h_new = _rglru(xb, gb, conv0, h0, conv_w[layer], conv_b[layer], wa_bd, b_rg_a[layer],
                                         wi_bd, b_rg_i[layer], lru_lambda[layer], n, l)
            oc, s_new = _retention(qc, kc, vc, gc, s0, n, l, pos0)
            x_mid, hn2, ti, tw = _merge(oa, ob, oc, gl, xs_path[path], g1, sc2, sh2, norm_ffn[layer], wbr, wo,
                                        wr_pad[layer], br_pad[layer].reshape(1, LANES), n_exp, tm)
            mid[path] = (x_mid, hn2, ti[:, :TOP_K], tw[:, :TOP_K], g2)
            states[path].append((st_k, st_v, st_f, conv_new, h_new.reshape(n, RG_WIDTH), s_new))

        hn2_all = jnp.concatenate([mid[0][1], mid[1][1]], axis=0)
        top_i = jnp.concatenate([mid[0][2], mid[1][2]], axis=0)
        top_w = jnp.concatenate([mid[0][3], mid[1][3]], axis=0)
        tile_expert, slot_token, slot_w, pos = _dispatch_plan(top_i, top_w, n_exp, ROW_TILE)
        xs_sorted = jnp.take(hn2_all, slot_token, axis=0, mode="clip")
        y = _experts(layer, tile_expert, xs_sorted, slot_w, wg_b, b_gate_e, wu_b, b_up_e, wd_b, b_down_e, ROW_TILE)
        yg = jnp.take(y, pos.T, axis=0, mode="clip")
        final = layer == depth - 1
        for path in (0, 1):
            n, l, tm, _ = geo[path]
            x_mid, _, _, _, g2 = mid[path]
            if final:
                gn, sc, sh = final_norm, g2, g2
            else:
                nxt = mods(layer + 1, path)
                gn, sc, sh = norm_mix[layer + 1], nxt[1], nxt[0]
            res = _combine(yg, 0 if path == 0 else rp // tm, x_mid, g2, gn, sc, sh, final, tm)
            if final:
                xs_path[path] = res[0]
            else:
                xs_path[path], hn_path[path] = res

    y_prompt = xs_path[0].reshape(n_p, seq, d)
    y_sample = xs_path[1].reshape(n_s, dec_seq, d)
    out = [y_prompt, y_sample]
    stack = lambda path, i: jnp.stack([s[i] for s in states[path]])
    out.append(jnp.transpose(stack(0, 0), (0, 1, 4, 2, 3)))
    out.append(jnp.transpose(stack(0, 1), (0, 1, 4, 2, 3)))
    out.append(jnp.transpose(stack(0, 2), (0, 1, 3, 2)))
    out += [stack(0, i) for i in range(3, 6)]
    out += [stack(1, i) for i in range(6)]
    return tuple(out)
```
